```python
import jax, jax.numpy as jnp
from jax import lax
import numpy as np

D_MODEL = 2048
BATCH = 8
SEQ = 2048
DEPTH = 2

HEAD_DIM = 128
GDN_HEADS = 8
SB_HEADS = 8
GDN_CONV = 4
GDN_CHUNK = 64
ATTN_HEADS = 16
ATTN_KV_HEADS = 4
IDX_HEADS = 16
IDX_DIM = 64
TOPK_MAX = 256
TOPK_KEY_FRACTION = 4
Q_BLOCK = 128
ROPE_THETA = 500000.0
ROPE_FRACTION = 4
D_FF = -(-8 * D_MODEL // (3 * 256)) * 256
NORM_EPS = 1e-6

GDN_WIDTH = GDN_HEADS * HEAD_DIM
SB_WIDTH = SB_HEADS * HEAD_DIM
AB_WIDTH = GDN_WIDTH + SB_WIDTH
AB_SPLITS = (3 * GDN_WIDTH, GDN_WIDTH, GDN_HEADS, GDN_HEADS, SB_WIDTH, SB_WIDTH, SB_WIDTH)
AB_IN = sum(AB_SPLITS)
C_Q = ATTN_HEADS * HEAD_DIM
C_KV = ATTN_KV_HEADS * HEAD_DIM
C_SPLITS = (C_Q, C_KV, C_KV, IDX_HEADS * IDX_DIM, IDX_DIM, IDX_HEADS)
C_IN = sum(C_SPLITS)
N_EVEN = (DEPTH + 1) // 2
N_ODD = DEPTH // 2

kernel_name = "hybrid_gdn_stickbreak_dsa_block"


def rmsnorm(x, g):
    xf = x.astype(jnp.float32)
    y = xf * lax.rsqrt(jnp.mean(xf * xf, axis=-1, keepdims=True) + NORM_EPS)
    return (y * g.astype(jnp.float32)).astype(x.dtype)


def l2norm(x):
    xf = x.astype(jnp.float32)
    return xf * lax.rsqrt(jnp.sum(xf * xf, axis=-1, keepdims=True) + NORM_EPS)


def split_cols(t, sizes):
    offs = np.cumsum(sizes)[:-1]
    return jnp.split(t, [int(o) for o in offs], axis=-1)


def partial_rope(x, positions):
    d = x.shape[-1]
    r = d // ROPE_FRACTION
    half = r // 2
    inv_freq = ROPE_THETA ** (-jnp.arange(half, dtype=jnp.float32) / half)
    ang = positions.astype(jnp.float32)[..., None] * inv_freq
    cos = jnp.cos(ang)[:, :, None, :]
    sin = jnp.sin(ang)[:, :, None, :]
    xf = x.astype(jnp.float32)
    x1, x2, rest = xf[..., :half], xf[..., half:r], xf[..., r:]
    out = jnp.concatenate([x1 * cos - x2 * sin, x2 * cos + x1 * sin, rest], axis=-1)
    return out.astype(x.dtype)


def causal_depthwise_conv(x, w):
    K, C = w.shape
    return lax.conv_general_dilated(
        x, w[:, None, :].astype(x.dtype), window_strides=(1,), padding=[(K - 1, 0)],
        dimension_numbers=("NWC", "WIO", "NWC"), feature_group_count=C)


def gated_delta_rule_chunked(q, k, v, g, beta):
    B, S, H, dk = q.shape
    dv = v.shape[-1]
    C = GDN_CHUNK
    N = S // C
    f32 = jnp.float32

    def chunks(t):
        return t.astype(f32).reshape(B, N, C, H, -1).transpose(1, 0, 3, 2, 4)

    qc = chunks(q) * (dk ** -0.5)
    kc = chunks(k)
    vc = chunks(v)
    gc = g.astype(f32).reshape(B, N, C, H).transpose(1, 0, 3, 2)
    bc = beta.astype(f32).reshape(B, N, C, H).transpose(1, 0, 3, 2)
    gcum = jnp.cumsum(gc, axis=-1)
    tril_incl = jnp.tril(jnp.ones((C, C), dtype=bool))
    tril_strict = jnp.tril(jnp.ones((C, C), dtype=bool), -1)
    decay = jnp.exp(jnp.where(tril_incl, gcum[..., :, None] - gcum[..., None, :], -jnp.inf))
    kb = kc * bc[..., None]
    lower = jnp.where(tril_strict, jnp.einsum('nbhid,nbhjd->nbhij', kb, kc) * decay, 0.0)
    eye = jnp.eye(C, dtype=f32)
    rhs = jnp.concatenate([vc * bc[..., None], kb * jnp.exp(gcum)[..., None]], axis=-1)
    sol = lax.linalg.triangular_solve(eye + lower, rhs, left_side=True, lower=True,
                                      unit_diagonal=True)
    u, w = sol[..., :dv], sol[..., dv:]
    intra = jnp.einsum('nbhid,nbhjd->nbhij', qc, kc) * decay

    def step(state, inp):
        q_i, k_i, u_i, w_i, a_i, g_i = inp
        v_new = u_i - jnp.einsum('bhcd,bhde->bhce', w_i, state)
        o = (jnp.einsum('bhcd,bhde->bhce', q_i * jnp.exp(g_i)[..., None], state)
             + jnp.einsum('bhij,bhje->bhie', a_i, v_new))
        g_last = g_i[..., -1]
        k_dec = k_i * jnp.exp(g_last[..., None] - g_i)[..., None]
        state = state * jnp.exp(g_last)[..., None, None] + jnp.einsum('bhcd,bhce->bhde', k_dec, v_new)
        return state, o

    state0 = jnp.zeros((B, H, dk, dv), f32)
    _, o = lax.scan(step, state0, (qc, kc, u, w, intra, gcum))
    return o.transpose(1, 0, 3, 2, 4).reshape(B, S, H, dv)


def stick_breaking_attention(q, k, v):
    B, S, H, d = q.shape
    scale = d ** -0.5
    outs = []
    for blk in range(S // Q_BLOCK):
        q0 = blk * Q_BLOCK
        kv_len = q0 + Q_BLOCK
        qb = q[:, q0:kv_len].astype(jnp.float32)
        kb = k[:, :kv_len].astype(jnp.float32)
        vb = v[:, :kv_len].astype(jnp.float32)
        z = jnp.einsum('bqhd,bkhd->bhqk', qb, kb) * scale
        t_idx = q0 + jnp.arange(Q_BLOCK)[:, None]
        s_idx = jnp.arange(kv_len)[None, :]
        causal = s_idx < t_idx
        log_stay = jnp.where(causal, jax.nn.log_sigmoid(-z), 0.0)
        between = lax.cumsum(log_stay, axis=3, reverse=True) - log_stay
        a = jnp.where(causal, jnp.exp(jax.nn.log_sigmoid(z) + between), 0.0)
        outs.append(jnp.einsum('bhqk,bkhd->bqhd', a, vb))
    return jnp.concatenate(outs, axis=1).astype(q.dtype)


def dsa_sparse_attention(q, k, v, qi, ki, wi):
    B, S, H, d = q.shape
    Hkv = k.shape[2]
    G = H // Hkv
    topk = min(TOPK_MAX, S // TOPK_KEY_FRACTION)
    nb = S // Q_BLOCK
    scale = d ** -0.5
    kf = k.astype(jnp.float32)
    vf = v.astype(jnp.float32)
    kif = ki.astype(jnp.float32)

    def blockify(t):
        return t.reshape(B, nb, Q_BLOCK, *t.shape[2:]).swapaxes(0, 1)

    def one_block(args):
        blk, qb, qib, wib = args
        t_idx = blk * Q_BLOCK + jnp.arange(Q_BLOCK)
        causal = jnp.arange(S)[None, :] <= t_idx[:, None]
        idx_logits = jnp.einsum('bqhe,bse->bqhs', qib.astype(jnp.float32), kif)
        score = jnp.einsum('bqh,bqhs->bqs', wib.astype(jnp.float32), jax.nn.relu(idx_logits))
        score = jnp.where(causal[None], score, -jnp.inf)
        _, sel = lax.top_k(score, topk)
        valid = sel <= t_idx[None, :, None]
        k_sel = jax.vmap(lambda kk, ii: kk[ii])(kf, sel)
        v_sel = jax.vmap(lambda vv, ii: vv[ii])(vf, sel)
        qg = qb.astype(jnp.float32).reshape(B, Q_BLOCK, Hkv, G, d)
        logits = jnp.einsum('bqhgd,bqkhd->bqhgk', qg, k_sel) * scale
        logits = jnp.where(valid[:, :, None, None, :], logits, -jnp.inf)
        p = jax.nn.softmax(logits, axis=-1)
        o = jnp.einsum('bqhgk,bqkhd->bqhgd', p, v_sel)
        return o.reshape(B, Q_BLOCK, H, d)

    out = lax.map(one_block, (jnp.arange(nb), blockify(q), blockify(qi), blockify(wi)))
    return out.swapaxes(0, 1).reshape(B, S, H, d).astype(q.dtype)


def mixer_ab(h, w_in, conv_w, a_log, dt_bias, gdn_norm, w_out):
    B, S, _ = h.shape
    proj = h @ w_in
    a_qkv, a_gate, a_alpha, a_beta, b_q, b_k, b_v = split_cols(proj, AB_SPLITS)
    a_qkv = jax.nn.silu(causal_depthwise_conv(a_qkv, conv_w))
    aq, ak, av = [t.reshape(B, S, GDN_HEADS, HEAD_DIM) for t in jnp.split(a_qkv, 3, axis=-1)]
    beta = jax.nn.sigmoid(a_beta.astype(jnp.float32))
    g = -jnp.exp(a_log.astype(jnp.float32)) * jax.nn.softplus(
        a_alpha.astype(jnp.float32) + dt_bias.astype(jnp.float32))
    o_a = gated_delta_rule_chunked(l2norm(aq), l2norm(ak), av, g, beta).astype(h.dtype)
    o_a = rmsnorm(o_a, gdn_norm) * jax.nn.silu(a_gate.reshape(B, S, GDN_HEADS, HEAD_DIM))
    o_b = stick_breaking_attention(b_q.reshape(B, S, SB_HEADS, HEAD_DIM),
                                   b_k.reshape(B, S, SB_HEADS, HEAD_DIM),
                                   b_v.reshape(B, S, SB_HEADS, HEAD_DIM))
    o = jnp.concatenate([o_a.reshape(B, S, GDN_WIDTH), o_b.reshape(B, S, SB_WIDTH)], axis=-1)
    return o @ w_out


def mixer_c(h, positions, w_in, w_out):
    B, S, _ = h.shape
    proj = h @ w_in
    q, k, v, qi, ki, wi = split_cols(proj, C_SPLITS)
    q = partial_rope(q.reshape(B, S, ATTN_HEADS, HEAD_DIM), positions)
    k = partial_rope(k.reshape(B, S, ATTN_KV_HEADS, HEAD_DIM), positions)
    v = v.reshape(B, S, ATTN_KV_HEADS, HEAD_DIM)
    qi = partial_rope(qi.reshape(B, S, IDX_HEADS, IDX_DIM), positions)
    ki = partial_rope(ki.reshape(B, S, 1, IDX_DIM), positions)[:, :, 0]
    wi = wi * ((IDX_HEADS ** -0.5) * (IDX_DIM ** -0.5))
    o = dsa_sparse_attention(q, k, v, qi, ki, wi)
    return o.reshape(B, S, C_Q) @ w_out


def swiglu(h, w_gate, w_up, w_down):
    return (jax.nn.silu(h @ w_gate) * (h @ w_up)) @ w_down


def setup_inputs(seed: int = 0) -> dict:
    key = jax.random.key(seed)
    ks = jax.random.split(key, 20)
    f32 = jnp.float32

    def nrm(k, shape, fan_in):
        return jax.random.normal(k, shape, f32) * (fan_in ** -0.5)

    x = jax.random.normal(ks[0], (BATCH, SEQ, D_MODEL), f32)
    offs = jax.random.randint(ks[1], (BATCH, 1), 0, 4096, dtype=jnp.int32)
    positions = offs + jnp.arange(SEQ, dtype=jnp.int32)[None, :]
    norm_mix = 1.0 + 0.02 * jax.random.normal(ks[2], (DEPTH, D_MODEL), f32)
    norm_ffn = 1.0 + 0.02 * jax.random.normal(ks[3], (DEPTH, D_MODEL), f32)
    final_norm = 1.0 + 0.02 * jax.random.normal(ks[4], (D_MODEL,), f32)
    w_in_ab = nrm(ks[5], (N_EVEN, D_MODEL, AB_IN), D_MODEL)
    conv_w_a = nrm(ks[6], (N_EVEN, GDN_CONV, 3 * GDN_WIDTH), GDN_CONV)
    a_log = jnp.log(jax.random.uniform(ks[7], (N_EVEN, GDN_HEADS), f32, 1.0, 16.0))
    dt = jnp.exp(jax.random.uniform(ks[8], (N_EVEN, GDN_HEADS), f32, np.log(1e-3), np.log(1e-1)))
    dt_bias = dt + jnp.log(-jnp.expm1(-dt))
    gdn_norm = 1.0 + 0.02 * jax.random.normal(ks[9], (N_EVEN, HEAD_DIM), f32)
    w_out_ab = nrm(ks[10], (N_EVEN, AB_WIDTH, D_MODEL), AB_WIDTH)
    w_in_c = nrm(ks[11], (N_ODD, D_MODEL, C_IN), D_MODEL)
    w_out_c = nrm(ks[12], (N_ODD, C_Q, D_MODEL), C_Q)
    ffn_gate = nrm(ks[13], (DEPTH, D_MODEL, D_FF), D_MODEL)
    ffn_up = nrm(ks[14], (DEPTH, D_MODEL, D_FF), D_MODEL)
    ffn_down = nrm(ks[15], (DEPTH, D_FF, D_MODEL), D_FF)
    return {"x": x, "positions": positions, "norm_mix": norm_mix, "norm_ffn": norm_ffn,
            "final_norm": final_norm, "w_in_ab": w_in_ab, "conv_w_a": conv_w_a, "a_log": a_log,
            "dt_bias": dt_bias, "gdn_norm": gdn_norm, "w_out_ab": w_out_ab, "w_in_c": w_in_c,
            "w_out_c": w_out_c, "ffn_gate": ffn_gate, "ffn_up": ffn_up, "ffn_down": ffn_down}


def reference(x, positions, norm_mix, norm_ffn, final_norm, w_in_ab, conv_w_a, a_log, dt_bias,
              gdn_norm, w_out_ab, w_in_c, w_out_c, ffn_gate, ffn_up, ffn_down):
    for layer in range(DEPTH):
        h = rmsnorm(x, norm_mix[layer])
        j = layer // 2
        if layer % 2 == 0:
            x = x + mixer_ab(h, w_in_ab[j], conv_w_a[j], a_log[j], dt_bias[j], gdn_norm[j], w_out_ab[j])
        else:
            x = x + mixer_c(h, positions, w_in_c[j], w_out_c[j])
        h = rmsnorm(x, norm_ffn[layer])
        x = x + swiglu(h, ffn_gate[layer], ffn_up[layer], ffn_down[layer])
    return rmsnorm(x, final_norm)
```

```python
import functools

import numpy as np
import jax
import jax.numpy as jnp
from jax import lax
from jax.experimental import pallas as pl
from jax.experimental.pallas import tpu as pltpu

F32 = jnp.float32
BF16 = jnp.bfloat16
I32 = jnp.int32

HEAD_DIM = 128
GDN_HEADS = 8
SB_HEADS = 8
GDN_CONV = 4
ATTN_HEADS = 16
ATTN_KV_HEADS = 4
GQA_GROUP = ATTN_HEADS // ATTN_KV_HEADS
IDX_HEADS = 16
IDX_DIM = 64
TOPK_MAX = 256
TOPK_KEY_FRACTION = 4
ROPE_THETA = 500000.0
ROPE_FRACTION = 4
NORM_EPS = 1e-6

LANES = 128
VMEM_LIMIT = 56 * 1024 * 1024
NEG_INF = float("-inf")


def _params(sem):
    return pltpu.CompilerParams(dimension_semantics=sem, vmem_limit_bytes=VMEM_LIMIT)


def _dot(a, b):
    return jnp.dot(a, b, preferred_element_type=F32)


def _dot_nt(a, b):
    return lax.dot_general(a, b, (((1,), (1,)), ((), ())), preferred_element_type=F32)


def _split(a):
    hi = a.astype(BF16)
    lo = (a - hi.astype(F32)).astype(BF16)
    return hi, lo


def _dot_split(a, b):
    ah, al = _split(a)
    bh, bl = _split(b)
    return _dot(ah, bh) + _dot(ah, bl) + _dot(al, bh)


def _softplus(x):
    return jnp.maximum(x, 0.0) + jnp.log1p(jnp.exp(-jnp.abs(x)))


def _sigmoid(x):
    return 1.0 / (1.0 + jnp.exp(-x))


def _rms_matmul_kernel(x_ref, g_ref, w_ref, o_ref, h_ref):
    @pl.when(pl.program_id(1) == 0)
    def _():
        x = x_ref[...]
        var = jnp.mean(x * x, axis=-1, keepdims=True)
        h_ref[...] = (x * lax.rsqrt(var + NORM_EPS) * g_ref[...]).astype(BF16)

    o_ref[...] = _dot(h_ref[...], w_ref[...]).astype(o_ref.dtype)


def rms_matmul(x, g, w, out_dtype, tm, tn):
    T, D = x.shape
    N = w.shape[1]
    return pl.pallas_call(
        _rms_matmul_kernel,
        grid=(T // tm, N // tn),
        in_specs=[pl.BlockSpec((tm, D), lambda i, j: (i, 0)),
                  pl.BlockSpec((1, D), lambda i, j: (0, 0)),
                  pl.BlockSpec((D, tn), lambda i, j: (0, j))],
        out_specs=pl.BlockSpec((tm, tn), lambda i, j: (i, j)),
        out_shape=jax.ShapeDtypeStruct((T, N), out_dtype),
        scratch_shapes=[pltpu.VMEM((tm, D), BF16)],
        compiler_params=_params(("parallel", "arbitrary")),
        name="rms_matmul",
    )(x, g, w)


def _ffn_up_kernel(x_ref, g_ref, wg_ref, wu_ref, o_ref, h_ref):
    @pl.when(pl.program_id(1) == 0)
    def _():
        x = x_ref[...]
        var = jnp.mean(x * x, axis=-1, keepdims=True)
        h_ref[...] = (x * lax.rsqrt(var + NORM_EPS) * g_ref[...]).astype(BF16)

    h = h_ref[...]
    a = _dot(h, wg_ref[...])
    b = _dot(h, wu_ref[...])
    o_ref[...] = (a * _sigmoid(a) * b).astype(o_ref.dtype)


def ffn_gate_up(x, g, wg, wu, tm, tn):
    T, D = x.shape
    N = wg.shape[1]
    return pl.pallas_call(
        _ffn_up_kernel,
        grid=(T // tm, N // tn),
        in_specs=[pl.BlockSpec((tm, D), lambda i, j: (i, 0)),
                  pl.BlockSpec((1, D), lambda i, j: (0, 0)),
                  pl.BlockSpec((D, tn), lambda i, j: (0, j)),
                  pl.BlockSpec((D, tn), lambda i, j: (0, j))],
        out_specs=pl.BlockSpec((tm, tn), lambda i, j: (i, j)),
        out_shape=jax.ShapeDtypeStruct((T, N), BF16),
        scratch_shapes=[pltpu.VMEM((tm, D), BF16)],
        compiler_params=_params(("parallel", "arbitrary")),
        name="ffn_up",
    )(x, g, wg, wu)


def _matmul_res_kernel(a_ref, w_ref, r_ref, o_ref):
    o_ref[...] = r_ref[...] + _dot(a_ref[...], w_ref[...])


def matmul_residual(a, w, res, tm, tn):
    T, K = a.shape
    N = w.shape[1]
    return pl.pallas_call(
        _matmul_res_kernel,
        grid=(T // tm, N // tn),
        in_specs=[pl.BlockSpec((tm, K), lambda i, j: (i, 0)),
                  pl.BlockSpec((K, tn), lambda i, j: (0, j)),
                  pl.BlockSpec((tm, tn), lambda i, j: (i, j))],
        out_specs=pl.BlockSpec((tm, tn), lambda i, j: (i, j)),
        out_shape=jax.ShapeDtypeStruct((T, N), F32),
        compiler_params=_params(("parallel", "arbitrary")),
        name="matmul_residual",
    )(a, w, res)


def _rmsnorm_kernel(x_ref, g_ref, o_ref):
    x = x_ref[...]
    var = jnp.mean(x * x, axis=-1, keepdims=True)
    o_ref[...] = x * lax.rsqrt(var + NORM_EPS) * g_ref[...]


def rmsnorm_rows(x, g, tm):
    T, D = x.shape
    return pl.pallas_call(
        _rmsnorm_kernel,
        grid=(T // tm,),
        in_specs=[pl.BlockSpec((tm, D), lambda i: (i, 0)),
                  pl.BlockSpec((1, D), lambda i: (0, 0))],
        out_specs=pl.BlockSpec((tm, D), lambda i: (i, 0)),
        out_shape=jax.ShapeDtypeStruct((T, D), F32),
        compiler_params=_params(("parallel",)),
        name="final_rmsnorm",
    )(x, g)


GDN_CHUNK = 128


def _gdn_kernel(q_ref, k_ref, v_ref, gate_ref, ab_ref, cwq_ref, cwk_ref, cwv_ref,
                alog_ref, dtb_ref, gn_ref, o_ref,
                qn, kn, vn, gb, bb, us, ws, qds, kdts, ints, egl):
    C = GDN_CHUNK
    S = q_ref.shape[1]
    n_chunks = S // C
    head = pl.program_id(1)
    scale = HEAD_DIM ** -0.5

    row_s = lax.broadcasted_iota(I32, (S, LANES), 0)

    def conv_silu(x_ref, cw_ref):
        x = x_ref[0].astype(F32)
        cw = cw_ref[...]
        y = x * cw[GDN_CONV - 1:GDN_CONV, :]
        for s in range(1, GDN_CONV):
            xs = jnp.where(row_s >= s, pltpu.roll(x, s, 0), 0.0)
            y = y + xs * cw[GDN_CONV - 1 - s:GDN_CONV - s, :]
        return y * _sigmoid(y)

    def l2n(x):
        return x * lax.rsqrt(jnp.sum(x * x, axis=-1, keepdims=True) + NORM_EPS)

    qn[...] = l2n(conv_silu(q_ref, cwq_ref))
    kn[...] = l2n(conv_silu(k_ref, cwk_ref))
    vn[...] = conv_silu(v_ref, cwv_ref)

    ab = ab_ref[0]
    lane_s = lax.broadcasted_iota(I32, (S, LANES), 1)
    g_all = -jnp.exp(alog_ref[...]) * _softplus(ab + dtb_ref[...])
    b_all = _sigmoid(ab)
    g_col = jnp.sum(jnp.where(lane_s == head, g_all, 0.0), axis=1, keepdims=True)
    b_col = jnp.sum(jnp.where(lane_s == head + GDN_HEADS, b_all, 0.0), axis=1, keepdims=True)
    gb[...] = jnp.broadcast_to(g_col, (S, LANES))
    bb[...] = jnp.broadcast_to(b_col, (S, LANES))

    row = lax.broadcasted_iota(I32, (C, C), 0)
    col = lax.broadcasted_iota(I32, (C, C), 1)
    eye = (row == col).astype(F32)

    def prep(c, carry):
        r0 = pl.multiple_of(c * C, C)
        sl = pl.ds(r0, C)
        gc = gb[sl, :]
        s = 1
        while s < C:
            gc = gc + jnp.where(row >= s, pltpu.roll(gc, s, 0), 0.0)
            s *= 2
        gct = gc.T
        decay = jnp.exp(jnp.where(row >= col, gc - gct, NEG_INF))
        kc = kn[sl, :]
        qc = qn[sl, :] * scale
        bc = bb[sl, :]
        kb = kc * bc
        kc16 = kc.astype(BF16)
        lower = jnp.where(row > col, _dot_nt(kb.astype(BF16), kc16) * decay, 0.0)
        intra = jnp.where(row >= col, _dot_nt(qc.astype(BF16), kc16) * decay, 0.0)
        p = lower
        t = eye - lower
        s = 2
        while s < C:
            p = _dot_split(p, p)
            t = t + _dot_split(t, p)
            s *= 2
        egc = jnp.exp(gc)
        us[sl, :] = _dot_split(t, vn[sl, :] * bc)
        ws[sl, :] = _dot_split(t, kb * egc).astype(BF16)
        qds[sl, :] = (qc * egc).astype(BF16)
        g_last = gc[C - 1:C, :]
        kdts[sl, :] = (kc * jnp.exp(g_last - gc)).T.astype(BF16)
        ints[sl, :] = intra.astype(BF16)
        egl[pl.ds(pl.multiple_of(c * 8, 8), 8), :] = jnp.broadcast_to(jnp.exp(g_last), (8, LANES))
        return carry

    lax.fori_loop(0, n_chunks, prep, 0)

    gn = gn_ref[...]

    def step(c, state):
        r0 = pl.multiple_of(c * C, C)
        sl = pl.ds(r0, C)
        s16 = state.astype(BF16)
        v_new = us[sl, :] - _dot(ws[sl, :], s16)
        vn16 = v_new.astype(BF16)
        o = _dot(qds[sl, :], s16) + _dot(ints[sl, :], vn16)
        eg = egl[pl.ds(pl.multiple_of(c * 8, 8), 8), :][0:1, :]
        state = state * eg + _dot(kdts[sl, :], vn16)
        var = jnp.mean(o * o, axis=-1, keepdims=True)
        gate = gate_ref[0, sl, :].astype(F32)
        o_ref[0, sl, :] = (o * lax.rsqrt(var + NORM_EPS) * gn * (gate * _sigmoid(gate))).astype(o_ref.dtype)
        return state

    lax.fori_loop(0, n_chunks, step, jnp.zeros((HEAD_DIM, HEAD_DIM), F32))


def gdn_mixer(proj, ab, conv_w, a_log, dt_bias, gdn_norm):
    B, S, _ = proj.shape
    H = GDN_HEADS
    blk = lambda off: pl.BlockSpec((1, S, HEAD_DIM), lambda b, h: (b, 0, off + h))
    cw = lambda off: pl.BlockSpec((GDN_CONV, HEAD_DIM), lambda b, h: (0, off + h))
    vec = pl.BlockSpec((1, LANES), lambda b, h: (0, 0))
    seq_f32 = pltpu.VMEM((S, HEAD_DIM), F32)
    seq_b16 = pltpu.VMEM((S, HEAD_DIM), BF16)
    return pl.pallas_call(
        _gdn_kernel,
        grid=(B, H),
        in_specs=[blk(0), blk(H), blk(2 * H), blk(3 * H),
                  pl.BlockSpec((1, S, LANES), lambda b, h: (b, 0, 0)),
                  cw(0), cw(H), cw(2 * H), vec, vec, vec],
        out_specs=pl.BlockSpec((1, S, HEAD_DIM), lambda b, h: (b, 0, h)),
        out_shape=jax.ShapeDtypeStruct((B, S, H * HEAD_DIM), BF16),
        scratch_shapes=[seq_f32, seq_f32, seq_f32, seq_f32, seq_f32, seq_f32,
                        seq_b16, seq_b16, seq_b16, seq_b16,
                        pltpu.VMEM((8 * (S // GDN_CHUNK), LANES), F32)],
        compiler_params=_params(("parallel", "arbitrary")),
        name="gdn_mixer",
    )(proj, proj, proj, proj, ab, conv_w, conv_w, conv_w, a_log, dt_bias, gdn_norm)


SB_TILE = 128


def _sb_kernel(q_ref, k_ref, v_ref, o_ref):
    t = SB_TILE
    qi = pl.program_id(2)
    scale = HEAD_DIM ** -0.5
    q = q_ref[0]
    row = lax.broadcasted_iota(I32, (t, t), 0)
    col = lax.broadcasted_iota(I32, (t, t), 1)
    after = (row > col).astype(BF16)

    def block(kb, vb, carry, acc, causal):
        z = _dot_nt(q, kb) * scale
        ls = -_softplus(z)
        if causal is not None:
            ls = jnp.where(causal, ls, 0.0)
        hi, lo = _split(ls)
        between = _dot(hi, after) + _dot(lo, after) + carry
        a = jnp.exp(z + ls + between)
        if causal is not None:
            a = jnp.where(causal, a, 0.0)
        acc = acc + _dot(a.astype(BF16), vb)
        carry = carry + jnp.sum(ls, axis=1, keepdims=True)
        return carry, acc

    r0 = pl.multiple_of(qi * t, t)
    carry, acc = block(k_ref[0, pl.ds(r0, t), :], v_ref[0, pl.ds(r0, t), :],
                       jnp.zeros((t, 1), F32), jnp.zeros((t, HEAD_DIM), F32), col < row)

    def body(i, ca):
        j = qi - 1 - i
        s0 = pl.multiple_of(j * t, t)
        return block(k_ref[0, pl.ds(s0, t), :], v_ref[0, pl.ds(s0, t), :], ca[0], ca[1], None)

    carry, acc = lax.fori_loop(0, qi, body, (carry, acc))
    o_ref[0] = acc.astype(o_ref.dtype)


def sb_attention(proj, col0):
    B, S, _ = proj.shape
    H = SB_HEADS
    t = SB_TILE
    return pl.pallas_call(
        _sb_kernel,
        grid=(B, H, S // t),
        in_specs=[pl.BlockSpec((1, t, HEAD_DIM), lambda b, h, i: (b, i, col0 + h)),
                  pl.BlockSpec((1, S, HEAD_DIM), lambda b, h, i: (b, 0, col0 + H + h)),
                  pl.BlockSpec((1, S, HEAD_DIM), lambda b, h, i: (b, 0, col0 + 2 * H + h))],
        out_specs=pl.BlockSpec((1, t, HEAD_DIM), lambda b, h, i: (b, i, h)),
        out_shape=jax.ShapeDtypeStruct((B, S, H * HEAD_DIM), BF16),
        compiler_params=_params(("parallel", "parallel", "arbitrary")),
        name="sb_attention",
    )(proj, proj, proj)


ROPE_TM = 512


def _rope_kernel(main_ref, small_ref, pos_ref, f_attn_ref, f_idx_ref,
                 q_ref, k_ref, v_ref, qi_ref, kilo_ref, kihi_ref, wi_ref):
    tm = main_ref.shape[0]
    pos = pos_ref[...]
    lane = lax.broadcasted_iota(I32, (tm, LANES), 1)

    def tables(f_ref, period, half):
        ang = pos * f_ref[...]
        c = jnp.cos(ang)
        s = jnp.sin(ang)
        p = lane % period
        return c, jnp.where(p >= half, s, 0.0), jnp.where(p < half, -s, 0.0)

    def rot(x, tab, half):
        c, s_up, s_dn = tab
        return x * c + pltpu.roll(x, half, 1) * s_up + pltpu.roll(x, LANES - half, 1) * s_dn

    half_a = HEAD_DIM // ROPE_FRACTION // 2
    half_i = IDX_DIM // ROPE_FRACTION // 2
    tab_a = tables(f_attn_ref, HEAD_DIM, half_a)
    tab_i = tables(f_idx_ref, IDX_DIM, half_i)

    nq = ATTN_HEADS
    nk = ATTN_KV_HEADS
    for c in range(nq):
        x = main_ref[:, c * LANES:(c + 1) * LANES].astype(F32)
        q_ref[:, c * LANES:(c + 1) * LANES] = rot(x, tab_a, half_a).astype(BF16)
    for c in range(nk):
        x = main_ref[:, (nq + c) * LANES:(nq + c + 1) * LANES].astype(F32)
        k_ref[:, c * LANES:(c + 1) * LANES] = rot(x, tab_a, half_a).astype(BF16)
    v_ref[...] = main_ref[:, (nq + nk) * LANES:(nq + 2 * nk) * LANES]
    c0 = nq + 2 * nk
    for c in range(IDX_HEADS * IDX_DIM // LANES):
        x = main_ref[:, (c0 + c) * LANES:(c0 + c + 1) * LANES].astype(F32)
        qi_ref[:, c * LANES:(c + 1) * LANES] = rot(x, tab_i, half_i).astype(BF16)
    sm = small_ref[...]
    ki = jnp.where(lane < IDX_DIM, rot(sm, tab_i, half_i), 0.0)
    kilo_ref[...] = ki.astype(BF16)
    kihi_ref[...] = pltpu.roll(ki, IDX_DIM, 1).astype(BF16)
    wi_scale = (IDX_HEADS ** -0.5) * (IDX_DIM ** -0.5)
    wi_ref[...] = jnp.where(lane < IDX_HEADS, pltpu.roll(sm, LANES - IDX_DIM, 1) * wi_scale, 0.0)


def rope_split(main, small, pos_b, f_attn, f_idx):
    T = main.shape[0]
    tm = ROPE_TM
    row = lambda n: pl.BlockSpec((tm, n), lambda i: (i, 0))
    vec = pl.BlockSpec((1, LANES), lambda i: (0, 0))
    nq, nk = ATTN_HEADS * HEAD_DIM, ATTN_KV_HEADS * HEAD_DIM
    ni = IDX_HEADS * IDX_DIM
    return pl.pallas_call(
        _rope_kernel,
        grid=(T // tm,),
        in_specs=[row(main.shape[1]), row(LANES), row(LANES), vec, vec],
        out_specs=[row(nq), row(nk), row(nk), row(ni), row(LANES), row(LANES), row(LANES)],
        out_shape=[jax.ShapeDtypeStruct((T, nq), BF16), jax.ShapeDtypeStruct((T, nk), BF16),
                   jax.ShapeDtypeStruct((T, nk), BF16), jax.ShapeDtypeStruct((T, ni), BF16),
                   jax.ShapeDtypeStruct((T, LANES), BF16), jax.ShapeDtypeStruct((T, LANES), BF16),
                   jax.ShapeDtypeStruct((T, LANES), F32)],
        compiler_params=_params(("parallel",)),
        name="rope_split",
    )(main, small, pos_b, f_attn, f_idx)


DSA_TQ = 128
DSA_TK = 256
INT_MIN = -2 ** 31


def _dsa_kernel(q_ref, k_ref, v_ref, qi_ref, kilo_ref, kihi_ref, wi_ref, o_ref,
                keys, m_s, l_s, acc_s):
    tq, tk = DSA_TQ, DSA_TK
    S = k_ref.shape[1]
    n_tiles = S // tk
    topk = min(TOPK_MAX, S // TOPK_KEY_FRACTION)
    qb = pl.program_id(1)
    q0 = qb * tq
    n_live = (q0 + tq + tk - 1) // tk
    scale = HEAD_DIM ** -0.5

    row_g = q0 + lax.broadcasted_iota(I32, (tq, tk), 0)
    col_l = lax.broadcasted_iota(I32, (tq, tk), 1)
    neg_key = jnp.full((tq, tk), INT_MIN + 0x7FFFFF, I32)

    wi = wi_ref[0]
    qi = qi_ref[0]

    def score_tile(kt, carry):
        s0 = pl.multiple_of(kt * tk, tk)

        @pl.when(kt < n_live)
        def _():
            klo = kilo_ref[0, pl.ds(s0, tk), :]
            khi = kihi_ref[0, pl.ds(s0, tk), :]
            score = jnp.zeros((tq, tk), F32)
            for h in range(IDX_HEADS):
                pair = qi[:, (h // 2) * LANES:(h // 2 + 1) * LANES]
                logit = _dot_nt(pair, klo if h % 2 == 0 else khi)
                score = score + wi[:, h:h + 1] * jnp.maximum(logit, 0.0)
            bits = pltpu.bitcast(score, I32)
            bits = jnp.where(bits == INT_MIN, 0, bits)
            key = bits ^ ((bits >> 31) & 0x7FFFFFFF)
            keys[:, pl.ds(s0, tk)] = jnp.where(s0 + col_l <= row_g, key, neg_key)

        @pl.when(kt >= n_live)
        def _():
            keys[:, pl.ds(s0, tk)] = neg_key

        return carry

    lax.fori_loop(0, n_tiles, score_tile, 0)

    kf = float(topk)

    def count_ge(cand):
        return jnp.sum(jnp.where(keys[...] >= cand, 1.0, 0.0), axis=1, keepdims=True)

    zero = jnp.zeros((tq, 1), I32)
    prefix = jnp.where(count_ge(zero) >= kf, zero, jnp.full((tq, 1), INT_MIN, I32))

    def bit_step(i, prefix):
        cand = prefix | jnp.left_shift(jnp.int32(1), 30 - i)
        return jnp.where(count_ge(cand) >= kf, cand, prefix)

    thr = lax.fori_loop(0, 31, bit_step, prefix)
    n_gt = jnp.sum(jnp.where(keys[...] > thr, 1.0, 0.0), axis=1, keepdims=True)
    need = kf - n_gt

    m_s[...] = jnp.full(m_s.shape, -1e30, F32)
    l_s[...] = jnp.zeros(l_s.shape, F32)
    acc_s[...] = jnp.zeros(acc_s.shape, F32)
    rr = lax.broadcasted_iota(I32, (tk, tk), 0)
    cc = lax.broadcasted_iota(I32, (tk, tk), 1)
    before = (rr < cc).astype(BF16)
    q_all = q_ref[0]

    def attn_tile(kt, ties_seen):
        s0 = pl.multiple_of(kt * tk, tk)
        key = keys[:, pl.ds(s0, tk)]
        eq = key == thr
        eq16 = jnp.where(eq, 1.0, 0.0).astype(BF16)
        rank = _dot(eq16, before) + ties_seen
        sel = ((key > thr) | (eq & (rank < need))) & (s0 + col_l <= row_g)
        for g in range(ATTN_KV_HEADS):
            kg = k_ref[0, pl.ds(s0, tk), g * HEAD_DIM:(g + 1) * HEAD_DIM]
            vg = v_ref[0, pl.ds(s0, tk), g * HEAD_DIM:(g + 1) * HEAD_DIM]
            for u in range(GQA_GROUP):
                h = g * GQA_GROUP + u
                qh = q_all[:, h * HEAD_DIM:(h + 1) * HEAD_DIM]
                s = jnp.where(sel, _dot_nt(qh, kg) * scale, -1e30)
                m_old = m_s[h]
                m_new = jnp.maximum(m_old, jnp.max(s, axis=1, keepdims=True))
                p = jnp.where(sel, jnp.exp(s - m_new), 0.0)
                alpha = jnp.exp(m_old - m_new)
                l_s[h] = alpha * l_s[h] + jnp.sum(p, axis=1, keepdims=True)
                acc_s[h] = alpha * acc_s[h] + _dot(p.astype(BF16), vg)
                m_s[h] = m_new
        return ties_seen + jnp.sum(jnp.where(eq, 1.0, 0.0), axis=1, keepdims=True)

    lax.fori_loop(0, n_live, attn_tile, jnp.zeros((tq, 1), F32))

    for h in range(ATTN_HEADS):
        o_ref[0, :, h * HEAD_DIM:(h + 1) * HEAD_DIM] = (acc_s[h] / l_s[h]).astype(o_ref.dtype)


def dsa_attention(q, k, v, qi, ki_lo, ki_hi, wi):
    B, S, _ = q.shape
    tq = DSA_TQ
    qblk = lambda n: pl.BlockSpec((1, tq, n), lambda b, i: (b, i, 0))
    seq = lambda n: pl.BlockSpec((1, S, n), lambda b, i: (b, 0, 0))
    return pl.pallas_call(
        _dsa_kernel,
        grid=(B, S // tq),
        in_specs=[qblk(q.shape[2]), seq(k.shape[2]), seq(v.shape[2]), qblk(qi.shape[2]),
                  seq(LANES), seq(LANES), qblk(LANES)],
        out_specs=qblk(q.shape[2]),
        out_shape=jax.ShapeDtypeStruct(q.shape, BF16),
        scratch_shapes=[pltpu.VMEM((tq, S), I32),
                        pltpu.VMEM((ATTN_HEADS, tq, 1), F32),
                        pltpu.VMEM((ATTN_HEADS, tq, 1), F32),
                        pltpu.VMEM((ATTN_HEADS, tq, HEAD_DIM), F32)],
        compiler_params=_params(("parallel", "arbitrary")),
        name="dsa_attention",
    )(q, k, v, qi, ki_lo, ki_hi, wi)


MM_TM = 512
MM_TN = 512


def _pad_cols(w, n):
    return jnp.pad(w, ((0, 0), (0, n - w.shape[1])))


def _row(v):
    return v.reshape(1, -1).astype(F32)


def _lane_vec(v):
    return jnp.pad(v.astype(F32), (0, LANES - v.shape[0])).reshape(1, LANES)


def _rope_freqs(width, period):
    half = width // ROPE_FRACTION // 2
    inv_freq = ROPE_THETA ** (-jnp.arange(half, dtype=F32) / half)
    head = jnp.concatenate([inv_freq, inv_freq, jnp.zeros((period - 2 * half,), F32)])
    return jnp.tile(head, LANES // period).reshape(1, LANES)


def _ffn(x, g, w_gate, w_up, w_down):
    u = ffn_gate_up(x, _row(g), w_gate.astype(BF16), w_up.astype(BF16), MM_TM, MM_TN)
    return matmul_residual(u, w_down.astype(BF16), x, MM_TM, MM_TN)


def _mixer_ab(xf, B, S, norm_g, w_in, conv_w, a_log, dt_bias, gdn_norm, w_out):
    T = B * S
    gw = GDN_HEADS * HEAD_DIM
    sw = SB_HEADS * HEAD_DIM
    n_small0 = 4 * gw
    w_main = jnp.concatenate([w_in[:, :n_small0], w_in[:, n_small0 + 2 * GDN_HEADS:]], axis=1).astype(BF16)
    w_small = _pad_cols(w_in[:, n_small0:n_small0 + 2 * GDN_HEADS], LANES).astype(BF16)
    g0 = _row(norm_g)
    proj = rms_matmul(xf, g0, w_main, BF16, MM_TM, MM_TN).reshape(B, S, -1)
    ab = rms_matmul(xf, g0, w_small, F32, MM_TM, LANES).reshape(B, S, LANES)
    o_a = gdn_mixer(proj, ab, conv_w, _lane_vec(a_log), _lane_vec(dt_bias), _row(gdn_norm))
    o_b = sb_attention(proj, (4 * gw) // HEAD_DIM)
    o = jnp.concatenate([o_a, o_b], axis=-1).reshape(T, gw + sw)
    return matmul_residual(o, w_out.astype(BF16), xf, MM_TM, MM_TN)


def _mixer_c(xf, B, S, positions, norm_g, w_in, w_out):
    T = B * S
    n_main = (ATTN_HEADS + 2 * ATTN_KV_HEADS) * HEAD_DIM + IDX_HEADS * IDX_DIM
    g1 = _row(norm_g)
    main = rms_matmul(xf, g1, w_in[:, :n_main].astype(BF16), BF16, MM_TM, MM_TN)
    small = rms_matmul(xf, g1, _pad_cols(w_in[:, n_main:], LANES).astype(BF16), F32, MM_TM, LANES)
    pos_b = jnp.broadcast_to(positions.reshape(T, 1).astype(F32), (T, LANES))
    q, k, v, qi, ki_lo, ki_hi, wi = rope_split(main, small, pos_b, _rope_freqs(HEAD_DIM, HEAD_DIM),
                                               _rope_freqs(IDX_DIM, IDX_DIM))
    r3 = lambda t: t.reshape(B, S, -1)
    o = dsa_attention(r3(q), r3(k), r3(v), r3(qi), r3(ki_lo), r3(ki_hi), r3(wi)).reshape(T, -1)
    return matmul_residual(o, w_out.astype(BF16), xf, MM_TM, MM_TN)


def kernel(x, positions, norm_mix, norm_ffn, final_norm, w_in_ab, conv_w_a, a_log, dt_bias, gdn_norm,
           w_out_ab, w_in_c, w_out_c, ffn_gate, ffn_up, ffn_down):
    B, S, D = x.shape
    xf = x.reshape(B * S, D)
    xf = _mixer_ab(xf, B, S, norm_mix[0], w_in_ab[0], conv_w_a[0], a_log[0], dt_bias[0], gdn_norm[0], w_out_ab[0])
    xf = _ffn(xf, norm_ffn[0], ffn_gate[0], ffn_up[0], ffn_down[0])
    xf = _mixer_c(xf, B, S, positions, norm_mix[1], w_in_c[0], w_out_c[0])
    xf = _ffn(xf, norm_ffn[1], ffn_gate[1], ffn_up[1], ffn_down[1])
    return rmsnorm_rows(xf, _row(final_norm), MM_TM).reshape(B, S, D)
```

```python
import functools

import numpy as np
import jax
import jax.numpy as jnp
from jax import lax
from jax.experimental import pallas as pl
from jax.experimental.pallas import tpu as pltpu

F32 = jnp.float32
BF16 = jnp.bfloat16
I32 = jnp.int32

HEAD_DIM = 128
GDN_HEADS = 8
SB_HEADS = 8
GDN_CONV = 4
ATTN_HEADS = 16
ATTN_KV_HEADS = 4
GQA_GROUP = ATTN_HEADS // ATTN_KV_HEADS
IDX_HEADS = 16
IDX_DIM = 64
TOPK_MAX = 256
TOPK_KEY_FRACTION = 4
ROPE_THETA = 500000.0
ROPE_FRACTION = 4
NORM_EPS = 1e-6

LANES = 128
VMEM_LIMIT = 56 * 1024 * 1024
NEG_INF = float("-inf")


def _params(sem):
    return pltpu.CompilerParams(dimension_semantics=sem, vmem_limit_bytes=VMEM_LIMIT)


def _dot(a, b):
    return jnp.dot(a, b, preferred_element_type=F32)


def _dot_nt(a, b):
    return lax.dot_general(a, b, (((1,), (1,)), ((), ())), preferred_element_type=F32)


def _split(a):
    hi = a.astype(BF16)
    lo = (a - hi.astype(F32)).astype(BF16)
    return hi, lo


def _dot_split(a, b):
    ah, al = _split(a)
    bh, bl = _split(b)
    return _dot(ah, bh) + _dot(ah, bl) + _dot(al, bh)


def _softplus(x):
    return jnp.maximum(x, 0.0) + jnp.log1p(jnp.exp(-jnp.abs(x)))


def _sigmoid(x):
    return 1.0 / (1.0 + jnp.exp(-x))


def _rms_matmul_kernel(x_ref, g_ref, w_ref, ws_ref, o_ref, os_ref, h_ref):
    @pl.when(pl.program_id(1) == 0)
    def _():
        x = x_ref[...]
        var = jnp.mean(x * x, axis=-1, keepdims=True)
        h_ref[...] = (x * lax.rsqrt(var + NORM_EPS) * g_ref[...]).astype(BF16)
        os_ref[...] = _dot(h_ref[...], ws_ref[...])

    o_ref[...] = _dot(h_ref[...], w_ref[...]).astype(o_ref.dtype)


def rms_matmul(x, g, w, w_small, tm, tn):
    T, D = x.shape
    tm = min(tm, T)
    N = w.shape[1]
    return pl.pallas_call(
        _rms_matmul_kernel,
        grid=(T // tm, N // tn),
        in_specs=[pl.BlockSpec((tm, D), lambda i, j: (i, 0)),
                  pl.BlockSpec((1, D), lambda i, j: (0, 0)),
                  pl.BlockSpec((D, tn), lambda i, j: (0, j)),
                  pl.BlockSpec((D, LANES), lambda i, j: (0, 0))],
        out_specs=[pl.BlockSpec((tm, tn), lambda i, j: (i, j)),
                   pl.BlockSpec((tm, LANES), lambda i, j: (i, 0))],
        out_shape=[jax.ShapeDtypeStruct((T, N), BF16), jax.ShapeDtypeStruct((T, LANES), F32)],
        scratch_shapes=[pltpu.VMEM((tm, D), BF16)],
        compiler_params=_params(("parallel", "arbitrary")),
        name="rms_matmul",
    )(x, g, w, w_small)


def _ffn_up_kernel(x_ref, g_ref, wg_ref, wu_ref, o_ref, h_ref):
    @pl.when(pl.program_id(1) == 0)
    def _():
        x = x_ref[...]
        var = jnp.mean(x * x, axis=-1, keepdims=True)
        h_ref[...] = (x * lax.rsqrt(var + NORM_EPS) * g_ref[...]).astype(BF16)

    h = h_ref[...]
    a = _dot(h, wg_ref[...])
    b = _dot(h, wu_ref[...])
    o_ref[...] = (a * _sigmoid(a) * b).astype(o_ref.dtype)


def ffn_gate_up(x, g, wg, wu, tm, tn):
    T, D = x.shape
    tm = min(tm, T)
    N = wg.shape[1]
    return pl.pallas_call(
        _ffn_up_kernel,
        grid=(T // tm, N // tn),
        in_specs=[pl.BlockSpec((tm, D), lambda i, j: (i, 0)),
                  pl.BlockSpec((1, D), lambda i, j: (0, 0)),
                  pl.BlockSpec((D, tn), lambda i, j: (0, j)),
                  pl.BlockSpec((D, tn), lambda i, j: (0, j))],
        out_specs=pl.BlockSpec((tm, tn), lambda i, j: (i, j)),
        out_shape=jax.ShapeDtypeStruct((T, N), BF16),
        scratch_shapes=[pltpu.VMEM((tm, D), BF16)],
        compiler_params=_params(("parallel", "arbitrary")),
        name="ffn_up",
    )(x, g, wg, wu)


def _matmul_res_kernel(a_ref, w_ref, r_ref, o_ref):
    o_ref[...] = r_ref[...] + _dot(a_ref[...], w_ref[...])


def matmul_residual(a, w, res, tm, tn):
    T, K = a.shape
    tm = min(tm, T)
    N = w.shape[1]
    return pl.pallas_call(
        _matmul_res_kernel,
        grid=(T // tm, N // tn),
        in_specs=[pl.BlockSpec((tm, K), lambda i, j: (i, 0)),
                  pl.BlockSpec((K, tn), lambda i, j: (0, j)),
                  pl.BlockSpec((tm, tn), lambda i, j: (i, j))],
        out_specs=pl.BlockSpec((tm, tn), lambda i, j: (i, j)),
        out_shape=jax.ShapeDtypeStruct((T, N), F32),
        compiler_params=_params(("parallel", "arbitrary")),
        name="matmul_residual",
    )(a, w, res)


def _rmsnorm_kernel(x_ref, g_ref, o_ref):
    x = x_ref[...]
    var = jnp.mean(x * x, axis=-1, keepdims=True)
    o_ref[...] = x * lax.rsqrt(var + NORM_EPS) * g_ref[...]


def rmsnorm_rows(x, g, tm):
    T, D = x.shape
    tm = min(tm, T)
    return pl.pallas_call(
        _rmsnorm_kernel,
        grid=(T // tm,),
        in_specs=[pl.BlockSpec((tm, D), lambda i: (i, 0)),
                  pl.BlockSpec((1, D), lambda i: (0, 0))],
        out_specs=pl.BlockSpec((tm, D), lambda i: (i, 0)),
        out_shape=jax.ShapeDtypeStruct((T, D), F32),
        compiler_params=_params(("parallel",)),
        name="final_rmsnorm",
    )(x, g)


GDN_CHUNK = 128
GDN_HEADS_PER_STEP = 2
GDN_PREP_GROUP = 4


def _gdn_kernel(q_ref, k_ref, v_ref, gate_ref, ab_ref, cwq_ref, cwk_ref, cwv_ref,
                alog_ref, dtb_ref, gn_ref, o_ref,
                qn, kn, vn, gb, bb, us, ws, qds, kdts, ints, egl):
    C = GDN_CHUNK
    hb = GDN_HEADS_PER_STEP
    cg = GDN_PREP_GROUP
    S = q_ref.shape[1]
    n_chunks = S // C
    n_groups = n_chunks // cg
    head0 = pl.program_id(1) * hb
    scale = HEAD_DIM ** -0.5
    heads = [slice(hh * HEAD_DIM, (hh + 1) * HEAD_DIM) for hh in range(hb)]

    row_s = lax.broadcasted_iota(I32, (S, LANES), 0)
    lane_s = lax.broadcasted_iota(I32, (S, LANES), 1)

    def conv_silu(x, cw):
        x = x.astype(F32)
        y = x * cw[GDN_CONV - 1:GDN_CONV, :]
        for s in range(1, GDN_CONV):
            xs = jnp.where(row_s >= s, pltpu.roll(x, s, 0), 0.0)
            y = y + xs * cw[GDN_CONV - 1 - s:GDN_CONV - s, :]
        return y * _sigmoid(y)

    def l2n(x):
        return x * lax.rsqrt(jnp.sum(x * x, axis=-1, keepdims=True) + NORM_EPS)

    ab = ab_ref[0]
    g_all = -jnp.exp(alog_ref[...]) * _softplus(ab + dtb_ref[...])
    b_all = _sigmoid(ab)
    for hh, hs in enumerate(heads):
        qn[hh] = l2n(conv_silu(q_ref[0, :, hs], cwq_ref[:, hs]))
        kn[hh] = l2n(conv_silu(k_ref[0, :, hs], cwk_ref[:, hs]))
        vn[hh] = conv_silu(v_ref[0, :, hs], cwv_ref[:, hs])
        g_col = jnp.sum(jnp.where(lane_s == head0 + hh, g_all, 0.0), axis=1, keepdims=True)
        b_col = jnp.sum(jnp.where(lane_s == head0 + hh + GDN_HEADS, b_all, 0.0), axis=1, keepdims=True)
        gb[hh] = jnp.broadcast_to(g_col, (S, LANES))
        bb[hh] = jnp.broadcast_to(b_col, (S, LANES))

    row = lax.broadcasted_iota(I32, (C, C), 0)
    col = lax.broadcasted_iota(I32, (C, C), 1)
    eye = (row == col).astype(F32)
    n_doublings = int(np.log2(C)) - 1

    def prep(i, carry):
        hh = i // n_groups
        c0 = (i % n_groups) * cg
        chunks = range(cg)
        sls = [pl.ds(pl.multiple_of((c0 + j) * C, C), C) for j in chunks]
        gcs = []
        for sl in sls:
            gc = gb[hh, sl, :]
            s = 1
            while s < C:
                gc = gc + jnp.where(row >= s, pltpu.roll(gc, s, 0), 0.0)
                s *= 2
            gcs.append(gc)
        decays = [jnp.exp(jnp.where(row >= col, gc - gc.T, NEG_INF)) for gc in gcs]
        kcs = [kn[hh, sl, :] for sl in sls]
        qcs = [qn[hh, sl, :] * scale for sl in sls]
        bcs = [bb[hh, sl, :] for sl in sls]
        kbs = [kcs[j] * bcs[j] for j in chunks]
        kc16 = [kc.astype(BF16) for kc in kcs]
        kk = [_dot_nt(kbs[j].astype(BF16), kc16[j]) for j in chunks]
        qk = [_dot_nt(qcs[j].astype(BF16), kc16[j]) for j in chunks]
        lowers = [jnp.where(row > col, kk[j] * decays[j], 0.0) for j in chunks]
        for j in chunks:
            ints[hh, sls[j], :] = jnp.where(row >= col, qk[j] * decays[j], 0.0).astype(BF16)
        ps = [_dot_split(l, l) for l in lowers]
        ts = [eye - l for l in lowers]
        for d in range(n_doublings):
            if d < n_doublings - 1:
                ys = [_dot_split(jnp.concatenate([ts[j], ps[j]], axis=0), ps[j]) for j in chunks]
                ts = [ts[j] + ys[j][:C] for j in chunks]
                ps = [ys[j][C:] for j in chunks]
            else:
                ys = [_dot_split(ts[j], ps[j]) for j in chunks]
                ts = [ts[j] + ys[j] for j in chunks]
        egcs = [jnp.exp(gc) for gc in gcs]
        rhs = [jnp.concatenate([vn[hh, sls[j], :] * bcs[j], kbs[j] * egcs[j]], axis=1) for j in chunks]
        sol = [_dot_split(ts[j], rhs[j]) for j in chunks]
        for j in chunks:
            sl = sls[j]
            us[hh, sl, :] = sol[j][:, :HEAD_DIM]
            ws[hh, sl, :] = sol[j][:, HEAD_DIM:].astype(BF16)
            qds[hh, sl, :] = (qcs[j] * egcs[j]).astype(BF16)
            g_last = gcs[j][C - 1:C, :]
            kdts[hh, sl, :] = (kcs[j] * jnp.exp(g_last - gcs[j])).T.astype(BF16)
            egl[hh, pl.ds(pl.multiple_of((c0 + j) * 8, 8), 8), :] = jnp.broadcast_to(jnp.exp(g_last), (8, LANES))
        return carry

    lax.fori_loop(0, hb * n_groups, prep, 0)

    gn = gn_ref[...]
    hrange = range(hb)

    def step(c, states):
        sl = pl.ds(pl.multiple_of(c * C, C), C)
        s16 = [st.astype(BF16) for st in states]
        w_s = [_dot(ws[hh, sl, :], s16[hh]) for hh in hrange]
        q_s = [_dot(qds[hh, sl, :], s16[hh]) for hh in hrange]
        vn16 = [(us[hh, sl, :] - w_s[hh]).astype(BF16) for hh in hrange]
        o_in = [_dot(ints[hh, sl, :], vn16[hh]) for hh in hrange]
        kv = [_dot(kdts[hh, sl, :], vn16[hh]) for hh in hrange]
        new_states = []
        for hh in hrange:
            eg = egl[hh, pl.ds(pl.multiple_of(c * 8, 8), 8), :][0:1, :]
            new_states.append(states[hh] * eg + kv[hh])
            o = q_s[hh] + o_in[hh]
            var = jnp.mean(o * o, axis=-1, keepdims=True)
            gate = gate_ref[0, sl, heads[hh]].astype(F32)
            o_ref[0, sl, heads[hh]] = (o * lax.rsqrt(var + NORM_EPS) * gn * (gate * _sigmoid(gate))).astype(o_ref.dtype)
        return tuple(new_states)

    lax.fori_loop(0, n_chunks, step, tuple(jnp.zeros((HEAD_DIM, HEAD_DIM), F32) for _ in hrange))


def gdn_mixer(proj, ab, conv_w, a_log, dt_bias, gdn_norm):
    B, S, _ = proj.shape
    hb = GDN_HEADS_PER_STEP
    n_grp = GDN_HEADS // hb
    w = hb * HEAD_DIM
    blk = lambda part: pl.BlockSpec((1, S, w), lambda b, g: (b, 0, part * n_grp + g))
    cw = lambda part: pl.BlockSpec((GDN_CONV, w), lambda b, g: (0, part * n_grp + g))
    vec = pl.BlockSpec((1, LANES), lambda b, g: (0, 0))
    seq_f32 = pltpu.VMEM((hb, S, HEAD_DIM), F32)
    seq_b16 = pltpu.VMEM((hb, S, HEAD_DIM), BF16)
    return pl.pallas_call(
        _gdn_kernel,
        grid=(B, n_grp),
        in_specs=[blk(0), blk(1), blk(2), blk(3),
                  pl.BlockSpec((1, S, LANES), lambda b, g: (b, 0, 0)),
                  cw(0), cw(1), cw(2), vec, vec, vec],
        out_specs=pl.BlockSpec((1, S, w), lambda b, g: (b, 0, g)),
        out_shape=jax.ShapeDtypeStruct((B, S, GDN_HEADS * HEAD_DIM), BF16),
        scratch_shapes=[seq_f32, seq_f32, seq_f32, seq_f32, seq_f32, seq_f32,
                        seq_b16, seq_b16, seq_b16, seq_b16,
                        pltpu.VMEM((hb, 8 * (S // GDN_CHUNK), LANES), F32)],
        compiler_params=_params(("parallel", "arbitrary")),
        name="gdn_mixer",
    )(proj, proj, proj, proj, ab, conv_w, conv_w, conv_w, a_log, dt_bias, gdn_norm)


SB_TILE = 128
SB_HEADS_PER_STEP = 8
SB_DEAD = -104.0


def _sb_kernel(q_ref, k_ref, v_ref, o_ref, carry_s, acc_s):
    t = SB_TILE
    hb = SB_HEADS_PER_STEP
    qi = pl.program_id(1)
    scale = HEAD_DIM ** -0.5
    row2 = lax.broadcasted_iota(I32, (t, 2 * t), 0)
    col2 = lax.broadcasted_iota(I32, (t, 2 * t), 1)
    after_ones = jnp.where(col2 >= t, 1.0, jnp.where(row2 > col2, 1.0, 0.0)).astype(BF16)
    causal = lax.broadcasted_iota(I32, (t, t), 1) < lax.broadcasted_iota(I32, (t, t), 0)

    heads = [slice(hh * HEAD_DIM, (hh + 1) * HEAD_DIM) for hh in range(hb)]

    def blocks(s0, diag):
        zs = [_dot_nt(q_ref[0, :, hs], k_ref[0, pl.ds(s0, t), hs]) * scale for hs in heads]
        rs, es = [], []
        for z in zs:
            ls = -(jnp.maximum(z, 0.0) + jnp.log(1.0 + jnp.exp(-jnp.abs(z))))
            if diag:
                ls = jnp.where(causal, ls, 0.0)
            hi, lo = _split(ls)
            rs.append(_dot(hi, after_ones) + _dot(lo, after_ones))
            es.append(z + ls)
        pvs = []
        for hh, hs in enumerate(heads):
            between = rs[hh][:, :t] if diag else rs[hh][:, :t] + carry_s[hh]
            a = jnp.exp(es[hh] + between)
            if diag:
                a = jnp.where(causal, a, 0.0)
            pvs.append(_dot(a.astype(BF16), v_ref[0, pl.ds(s0, t), hs]))
        for hh in range(hb):
            if diag:
                acc_s[hh] = pvs[hh]
                carry_s[hh] = rs[hh][:, t:]
            else:
                acc_s[hh] += pvs[hh]
                carry_s[hh] += rs[hh][:, t:]

    def any_live():
        m = carry_s[0]
        for hh in range(1, hb):
            m = jnp.maximum(m, carry_s[hh])
        return jnp.max(m) >= SB_DEAD

    blocks(pl.multiple_of(qi * t, t), True)

    def cond(st):
        return (st[0] < qi) & st[1]

    def body(st):
        j = qi - 1 - st[0]
        blocks(pl.multiple_of(j * t, t), False)
        return st[0] + 1, any_live()

    lax.while_loop(cond, body, (jnp.int32(0), any_live()))
    for hh in range(hb):
        o_ref[0, :, hh * HEAD_DIM:(hh + 1) * HEAD_DIM] = acc_s[hh].astype(o_ref.dtype)


def sb_attention(proj, col0):
    B, S, _ = proj.shape
    hb = SB_HEADS_PER_STEP
    n_grp = SB_HEADS // hb
    t = SB_TILE
    w = hb * HEAD_DIM
    c0 = col0 // hb
    return pl.pallas_call(
        _sb_kernel,
        grid=(B * n_grp, S // t),
        in_specs=[pl.BlockSpec((1, t, w), lambda g, i: (g // n_grp, i, c0 + g % n_grp)),
                  pl.BlockSpec((1, S, w), lambda g, i: (g // n_grp, 0, c0 + n_grp + g % n_grp)),
                  pl.BlockSpec((1, S, w), lambda g, i: (g // n_grp, 0, c0 + 2 * n_grp + g % n_grp))],
        out_specs=pl.BlockSpec((1, t, w), lambda g, i: (g // n_grp, i, g % n_grp)),
        out_shape=jax.ShapeDtypeStruct((B, S, SB_HEADS * HEAD_DIM), BF16),
        scratch_shapes=[pltpu.VMEM((hb, t, t), F32), pltpu.VMEM((hb, t, HEAD_DIM), F32)],
        compiler_params=_params(("parallel", "arbitrary")),
        name="sb_attention",
    )(proj, proj, proj)


ROPE_TM = 512


def _rope_kernel(main_ref, small_ref, pos_ref, f_attn_ref, f_idx_ref,
                 q_ref, k_ref, vt_ref, qi_ref, kilo_ref, kihi_ref, wi_ref):
    tm = main_ref.shape[0]
    pos = pos_ref[...]
    lane = lax.broadcasted_iota(I32, (tm, LANES), 1)

    def tables(f_ref, period, half):
        ang = pos * f_ref[...]
        c = jnp.cos(ang)
        s = jnp.sin(ang)
        p = lane % period
        return c, jnp.where(p >= half, s, 0.0), jnp.where(p < half, -s, 0.0)

    def rot(x, tab, half):
        c, s_up, s_dn = tab
        return x * c + pltpu.roll(x, half, 1) * s_up + pltpu.roll(x, LANES - half, 1) * s_dn

    half_a = HEAD_DIM // ROPE_FRACTION // 2
    half_i = IDX_DIM // ROPE_FRACTION // 2
    tab_a = tables(f_attn_ref, HEAD_DIM, half_a)
    tab_i = tables(f_idx_ref, IDX_DIM, half_i)

    nq = ATTN_HEADS
    nk = ATTN_KV_HEADS
    for c in range(nq):
        x = main_ref[:, c * LANES:(c + 1) * LANES].astype(F32)
        q_ref[:, c * LANES:(c + 1) * LANES] = rot(x, tab_a, half_a).astype(BF16)
    for c in range(nk):
        x = main_ref[:, (nq + c) * LANES:(nq + c + 1) * LANES].astype(F32)
        k_ref[:, c * LANES:(c + 1) * LANES] = rot(x, tab_a, half_a).astype(BF16)
    vt_ref[0] = main_ref[:, (nq + nk) * LANES:(nq + 2 * nk) * LANES].astype(F32).T.astype(BF16)
    c0 = nq + 2 * nk
    for c in range(IDX_HEADS * IDX_DIM // LANES):
        x = main_ref[:, (c0 + c) * LANES:(c0 + c + 1) * LANES].astype(F32)
        qi_ref[:, c * LANES:(c + 1) * LANES] = rot(x, tab_i, half_i).astype(BF16)
    sm = small_ref[...]
    ki = jnp.where(lane < IDX_DIM, rot(sm, tab_i, half_i), 0.0)
    kilo_ref[...] = ki.astype(BF16)
    kihi_ref[...] = pltpu.roll(ki, IDX_DIM, 1).astype(BF16)
    wi_scale = (IDX_HEADS ** -0.5) * (IDX_DIM ** -0.5)
    wi_ref[...] = jnp.where(lane < IDX_HEADS, pltpu.roll(sm, LANES - IDX_DIM, 1) * wi_scale, 0.0)


def rope_split(main, small, pos_b, f_attn, f_idx, seq_len):
    T = main.shape[0]
    tm = ROPE_TM
    per_seq = seq_len // tm
    row = lambda n: pl.BlockSpec((tm, n), lambda i: (i, 0))
    vec = pl.BlockSpec((1, LANES), lambda i: (0, 0))
    nq, nk = ATTN_HEADS * HEAD_DIM, ATTN_KV_HEADS * HEAD_DIM
    ni = IDX_HEADS * IDX_DIM
    return pl.pallas_call(
        _rope_kernel,
        grid=(T // tm,),
        in_specs=[row(main.shape[1]), row(LANES), row(LANES), vec, vec],
        out_specs=[row(nq), row(nk), pl.BlockSpec((1, nk, tm), lambda i: (i // per_seq, 0, i % per_seq)),
                   row(ni), row(LANES), row(LANES), row(LANES)],
        out_shape=[jax.ShapeDtypeStruct((T, nq), BF16), jax.ShapeDtypeStruct((T, nk), BF16),
                   jax.ShapeDtypeStruct((T // seq_len, nk, seq_len), BF16), jax.ShapeDtypeStruct((T, ni), BF16),
                   jax.ShapeDtypeStruct((T, LANES), BF16), jax.ShapeDtypeStruct((T, LANES), BF16),
                   jax.ShapeDtypeStruct((T, LANES), F32)],
        compiler_params=_params(("parallel",)),
        name="rope_split",
    )(main, small, pos_b, f_attn, f_idx)


DSA_TQ = 128
DSA_TK = 256
INT_MIN = -2 ** 31


def _dsa_kernel(q_ref, k_ref, vt_ref, qi_ref, kilo_ref, kihi_ref, wi_ref, o_ref,
                scores, m_s, l_s, acc_s):
    tq, tk = DSA_TQ, DSA_TK
    S = k_ref.shape[1]
    topk = min(TOPK_MAX, S // TOPK_KEY_FRACTION)
    q0 = pl.program_id(1) * tq
    n_live = (q0 + tq + tk - 1) // tk
    exp2_scale = (HEAD_DIM ** -0.5) * float(np.log2(np.e))

    key_i = lax.broadcasted_iota(I32, (tk, tq), 0)
    qry_i = q0 + lax.broadcasted_iota(I32, (tk, tq), 1)
    neg_key = INT_MIN + 0x7FFFFF

    def tile_start(kt):
        return pl.multiple_of(kt * tk, tk)

    def key_to_float(key):
        return pltpu.bitcast(jnp.where(key < 0, key ^ 0x7FFFFFFF, key), F32)

    wit = wi_ref[0].T
    qi_pairs = jnp.concatenate([qi_ref[0, :, c * LANES:(c + 1) * LANES]
                                for c in range(IDX_HEADS * IDX_DIM // LANES)], axis=0)

    def score_tile(kt, carry):
        s0 = tile_start(kt)
        even = _dot_nt(kilo_ref[0, pl.ds(s0, tk), :], qi_pairs)
        odd = _dot_nt(kihi_ref[0, pl.ds(s0, tk), :], qi_pairs)
        score = None
        for h in range(IDX_HEADS):
            logit = (even if h % 2 == 0 else odd)[:, (h // 2) * tq:(h // 2 + 1) * tq]
            term = wit[h:h + 1, :] * jnp.maximum(logit, 0.0)
            score = term if score is None else score + term
        scores[pl.ds(s0, tk), :] = jnp.where(s0 + key_i <= qry_i, score, NEG_INF)
        return carry

    lax.fori_loop(0, n_live, score_tile, 0)

    @pl.when(n_live % 2 == 1)
    def _():
        scores[pl.ds(tile_start(n_live), tk), :] = jnp.full((tk, tq), NEG_INF, F32)

    kf = float(topk)
    ts = 2 * tk
    n_sums = 4

    def count(above):
        def tile(kt, acc):
            hit = jnp.where(above(scores[pl.ds(pl.multiple_of(kt * ts, ts), ts), :]), 1.0, 0.0)
            return acc + jnp.sum(hit.reshape(n_sums, ts // (8 * n_sums), 8, tq), axis=1)
        acc = lax.fori_loop(0, (n_live + 1) // 2, tile, jnp.zeros((n_sums, 8, tq), F32))
        return jnp.sum(acc.reshape(n_sums * 8, tq), axis=0, keepdims=True)

    zero = jnp.zeros((1, tq), I32)
    prefix = jnp.where(count(lambda s: s >= 0.0) >= kf, zero, jnp.full((1, tq), INT_MIN, I32))

    def bit_step(i, prefix):
        cand = prefix | jnp.left_shift(jnp.int32(1), 30 - i)
        cand_f = key_to_float(cand)
        return jnp.where(count(lambda s: s >= cand_f) >= kf, cand, prefix)

    thr_key = lax.fori_loop(0, 31, bit_step, prefix)
    thr = jnp.where(thr_key < neg_key, NEG_INF, key_to_float(thr_key))
    need = kf - count(lambda s: s > thr)

    m_s[...] = jnp.full(m_s.shape, -1e30, F32)
    l_s[...] = jnp.zeros(l_s.shape, F32)
    acc_s[...] = jnp.zeros(acc_s.shape, F32)
    before = (lax.broadcasted_iota(I32, (tk, tk), 1) < lax.broadcasted_iota(I32, (tk, tk), 0)).astype(BF16)
    groups = range(ATTN_KV_HEADS)
    q4 = [jnp.concatenate([q_ref[0, :, (g * GQA_GROUP + u) * HEAD_DIM:(g * GQA_GROUP + u + 1) * HEAD_DIM]
                           for u in range(GQA_GROUP)], axis=0) for g in groups]

    def attn_tile(kt, ties_seen):
        s0 = tile_start(kt)
        score = scores[pl.ds(s0, tk), :]
        eq = score == thr
        rank = _dot(before, jnp.where(eq, 1.0, 0.0).astype(BF16)) + ties_seen
        sel = ((score > thr) | (eq & (rank < need))) & (s0 + key_i <= qry_i)
        bias = jnp.where(sel, 0.0, -1e30)
        sts = [_dot_nt(k_ref[0, pl.ds(s0, tk), g * HEAD_DIM:(g + 1) * HEAD_DIM], q4[g]) for g in groups]
        ps, alphas = [], []
        for g in groups:
            p_g, a_g = [], []
            for u in range(GQA_GROUP):
                h = g * GQA_GROUP + u
                s = sts[g][:, u * tq:(u + 1) * tq] + bias
                m_old = m_s[h]
                m_new = jnp.maximum(m_old, jnp.max(s, axis=0, keepdims=True))
                p = jnp.exp2((s - m_new) * exp2_scale)
                alpha = jnp.exp2((m_old - m_new) * exp2_scale)
                l_s[h] = alpha * l_s[h] + jnp.sum(p, axis=0, keepdims=True)
                m_s[h] = m_new
                p_g.append(p.astype(BF16))
                a_g.append(alpha)
            ps.append(jnp.concatenate(p_g, axis=1))
            alphas.append(jnp.concatenate(a_g, axis=1))
        pvs = [_dot(vt_ref[0, g * HEAD_DIM:(g + 1) * HEAD_DIM, pl.ds(s0, tk)], ps[g]) for g in groups]
        for g in groups:
            acc_s[g] = alphas[g] * acc_s[g] + pvs[g]
        return ties_seen + jnp.sum(jnp.where(eq, 1.0, 0.0), axis=0, keepdims=True)

    lax.fori_loop(0, n_live, attn_tile, jnp.zeros((1, tq), F32))

    for g in groups:
        for u in range(GQA_GROUP):
            h = g * GQA_GROUP + u
            o_t = acc_s[g, :, u * tq:(u + 1) * tq] / l_s[h]
            o_ref[0, :, h * HEAD_DIM:(h + 1) * HEAD_DIM] = o_t.T.astype(o_ref.dtype)


def dsa_attention(q, k, vt, qi, ki_lo, ki_hi, wi):
    B, S, _ = q.shape
    tq = DSA_TQ
    qblk = lambda n: pl.BlockSpec((1, tq, n), lambda b, i: (b, i, 0))
    seq = lambda n: pl.BlockSpec((1, S, n), lambda b, i: (b, 0, 0))
    return pl.pallas_call(
        _dsa_kernel,
        grid=(B, S // tq),
        in_specs=[qblk(q.shape[2]), seq(k.shape[2]),
                  pl.BlockSpec((1, vt.shape[1], S), lambda b, i: (b, 0, 0)),
                  qblk(qi.shape[2]), seq(LANES), seq(LANES), qblk(LANES)],
        out_specs=qblk(q.shape[2]),
        out_shape=jax.ShapeDtypeStruct(q.shape, BF16),
        scratch_shapes=[pltpu.VMEM((S, tq), F32),
                        pltpu.VMEM((ATTN_HEADS, 1, tq), F32),
                        pltpu.VMEM((ATTN_HEADS, 1, tq), F32),
                        pltpu.VMEM((ATTN_KV_HEADS, HEAD_DIM, GQA_GROUP * tq), F32)],
        compiler_params=_params(("parallel", "arbitrary")),
        name="dsa_attention",
    )(q, k, vt, qi, ki_lo, ki_hi, wi)


MM_TM = 1024
MM_TN = 512


def _pad_cols(w, n):
    return jnp.pad(w, ((0, 0), (0, n - w.shape[1])))


def _row(v):
    return v.reshape(1, -1).astype(F32)


def _lane_vec(v):
    return jnp.pad(v.astype(F32), (0, LANES - v.shape[0])).reshape(1, LANES)


def _rope_freqs(width, period):
    half = width // ROPE_FRACTION // 2
    inv_freq = ROPE_THETA ** (-jnp.arange(half, dtype=F32) / half)
    head = jnp.concatenate([inv_freq, inv_freq, jnp.zeros((period - 2 * half,), F32)])
    return jnp.tile(head, LANES // period).reshape(1, LANES)


def _ffn(x, g, w_gate, w_up, w_down):
    u = ffn_gate_up(x, _row(g), w_gate.astype(BF16), w_up.astype(BF16), MM_TM, MM_TN)
    return matmul_residual(u, w_down.astype(BF16), x, MM_TM, MM_TN)


def _mixer_ab(xf, B, S, norm_g, w_in, conv_w, a_log, dt_bias, gdn_norm, w_out):
    T = B * S
    gw = GDN_HEADS * HEAD_DIM
    sw = SB_HEADS * HEAD_DIM
    n_small0 = 4 * gw
    w_main = jnp.concatenate([w_in[:, :n_small0], w_in[:, n_small0 + 2 * GDN_HEADS:]], axis=1).astype(BF16)
    w_small = _pad_cols(w_in[:, n_small0:n_small0 + 2 * GDN_HEADS], LANES).astype(BF16)
    g0 = _row(norm_g)
    proj, ab = rms_matmul(xf, g0, w_main, w_small, MM_TM, MM_TN)
    proj = proj.reshape(B, S, -1)
    ab = ab.reshape(B, S, LANES)
    o_a = gdn_mixer(proj, ab, conv_w, _lane_vec(a_log), _lane_vec(dt_bias), _row(gdn_norm))
    o_b = sb_attention(proj, (4 * gw) // HEAD_DIM)
    o = jnp.concatenate([o_a, o_b], axis=-1).reshape(T, gw + sw)
    return matmul_residual(o, w_out.astype(BF16), xf, MM_TM, MM_TN)


def _mixer_c(xf, B, S, positions, norm_g, w_in, w_out):
    T = B * S
    n_main = (ATTN_HEADS + 2 * ATTN_KV_HEADS) * HEAD_DIM + IDX_HEADS * IDX_DIM
    g1 = _row(norm_g)
    main, small = rms_matmul(xf, g1, w_in[:, :n_main].astype(BF16),
                             _pad_cols(w_in[:, n_main:], LANES).astype(BF16), MM_TM, MM_TN)
    pos_b = jnp.broadcast_to(positions.reshape(T, 1).astype(F32), (T, LANES))
    q, k, vt, qi, ki_lo, ki_hi, wi = rope_split(main, small, pos_b, _rope_freqs(HEAD_DIM, HEAD_DIM),
                                                _rope_freqs(IDX_DIM, IDX_DIM), S)
    r3 = lambda t: t.reshape(B, S, -1)
    o = dsa_attention(r3(q), r3(k), vt, r3(qi), r3(ki_lo), r3(ki_hi), r3(wi)).reshape(T, -1)
    return matmul_residual(o, w_out.astype(BF16), xf, MM_TM, MM_TN)


def kernel(x, positions, norm_mix, norm_ffn, final_norm, w_in_ab, conv_w_a, a_log, dt_bias, gdn_norm,
           w_out_ab, w_in_c, w_out_c, ffn_gate, ffn_up, ffn_down):
    B, S, D = x.shape
    xf = x.reshape(B * S, D)
    xf = _mixer_ab(xf, B, S, norm_mix[0], w_in_ab[0], conv_w_a[0], a_log[0], dt_bias[0], gdn_norm[0], w_out_ab[0])
    xf = _ffn(xf, norm_ffn[0], ffn_gate[0], ffn_up[0], ffn_down[0])
    xf = _mixer_c(xf, B, S, positions, norm_mix[1], w_in_c[0], w_out_c[0])
    xf = _ffn(xf, norm_ffn[1], ffn_gate[1], ffn_up[1], ffn_down[1])
    return rmsnorm_rows(xf, _row(final_norm), MM_TM).reshape(B, S, D)
```

```python
import functools

import numpy as np
import jax
import jax.numpy as jnp
from jax import lax
from jax.experimental import pallas as pl
from jax.experimental.pallas import tpu as pltpu

F32 = jnp.float32
BF16 = jnp.bfloat16
I32 = jnp.int32

HEAD_DIM = 128
GDN_HEADS = 8
SB_HEADS = 8
GDN_CONV = 4
ATTN_HEADS = 16
ATTN_KV_HEADS = 4
GQA_GROUP = ATTN_HEADS // ATTN_KV_HEADS
IDX_HEADS = 16
IDX_DIM = 64
TOPK_MAX = 256
TOPK_KEY_FRACTION = 4
ROPE_THETA = 500000.0
ROPE_FRACTION = 4
NORM_EPS = 1e-6

LANES = 128
VMEM_LIMIT = 56 * 1024 * 1024
NEG_INF = float("-inf")


def _params(sem):
    return pltpu.CompilerParams(dimension_semantics=sem, vmem_limit_bytes=VMEM_LIMIT)


def _dot(a, b):
    return jnp.dot(a, b, preferred_element_type=F32)


def _dot_nt(a, b):
    return lax.dot_general(a, b, (((1,), (1,)), ((), ())), preferred_element_type=F32)


def _split(a):
    hi = a.astype(BF16)
    lo = (a - hi.astype(F32)).astype(BF16)
    return hi, lo


def _dot_split(a, b):
    ah, al = _split(a)
    bh, bl = _split(b)
    return _dot(jnp.concatenate([ah, al], axis=1), jnp.concatenate([bh, bh], axis=0)) + _dot(ah, bl)


def _softplus(x):
    return jnp.maximum(x, 0.0) + jnp.log1p(jnp.exp(-jnp.abs(x)))


def _sigmoid(x):
    return 1.0 / (1.0 + jnp.exp(-x))


def _rms_matmul_kernel(x_ref, g_ref, w_ref, ws_ref, o_ref, os_ref, h_ref):
    @pl.when(pl.program_id(1) == 0)
    def _():
        x = x_ref[...]
        var = jnp.mean(x * x, axis=-1, keepdims=True)
        h_ref[...] = (x * lax.rsqrt(var + NORM_EPS) * g_ref[...]).astype(BF16)
        os_ref[...] = _dot(h_ref[...], ws_ref[...])

    o_ref[...] = _dot(h_ref[...], w_ref[...]).astype(o_ref.dtype)


def rms_matmul(x, g, w, w_small, tm, tn):
    T, D = x.shape
    tm = min(tm, T)
    N = w.shape[1]
    return pl.pallas_call(
        _rms_matmul_kernel,
        grid=(T // tm, N // tn),
        in_specs=[pl.BlockSpec((tm, D), lambda i, j: (i, 0)),
                  pl.BlockSpec((1, D), lambda i, j: (0, 0)),
                  pl.BlockSpec((D, tn), lambda i, j: (0, j)),
                  pl.BlockSpec((D, LANES), lambda i, j: (0, 0))],
        out_specs=[pl.BlockSpec((tm, tn), lambda i, j: (i, j)),
                   pl.BlockSpec((tm, LANES), lambda i, j: (i, 0))],
        out_shape=[jax.ShapeDtypeStruct((T, N), BF16), jax.ShapeDtypeStruct((T, LANES), F32)],
        scratch_shapes=[pltpu.VMEM((tm, D), BF16)],
        compiler_params=_params(("parallel", "arbitrary")),
        name="rms_matmul",
    )(x, g, w, w_small)


def _ffn_up_kernel(x_ref, g_ref, wg_ref, wu_ref, o_ref, h_ref):
    @pl.when(pl.program_id(1) == 0)
    def _():
        x = x_ref[...]
        var = jnp.mean(x * x, axis=-1, keepdims=True)
        h_ref[...] = (x * lax.rsqrt(var + NORM_EPS) * g_ref[...]).astype(BF16)

    h = h_ref[...]
    a = _dot(h, wg_ref[...])
    b = _dot(h, wu_ref[...])
    o_ref[...] = (a * _sigmoid(a) * b).astype(o_ref.dtype)


def ffn_gate_up(x, g, wg, wu, tm, tn):
    T, D = x.shape
    tm = min(tm, T)
    N = wg.shape[1]
    return pl.pallas_call(
        _ffn_up_kernel,
        grid=(T // tm, N // tn),
        in_specs=[pl.BlockSpec((tm, D), lambda i, j: (i, 0)),
                  pl.BlockSpec((1, D), lambda i, j: (0, 0)),
                  pl.BlockSpec((D, tn), lambda i, j: (0, j)),
                  pl.BlockSpec((D, tn), lambda i, j: (0, j))],
        out_specs=pl.BlockSpec((tm, tn), lambda i, j: (i, j)),
        out_shape=jax.ShapeDtypeStruct((T, N), BF16),
        scratch_shapes=[pltpu.VMEM((tm, D), BF16)],
        compiler_params=_params(("parallel", "arbitrary")),
        name="ffn_up",
    )(x, g, wg, wu)


def _matmul_res_kernel(a_ref, w_ref, r_ref, o_ref):
    o_ref[...] = r_ref[...] + _dot(a_ref[...], w_ref[...])


def matmul_residual(a, w, res, tm, tn):
    T, K = a.shape
    tm = min(tm, T)
    N = w.shape[1]
    return pl.pallas_call(
        _matmul_res_kernel,
        grid=(T // tm, N // tn),
        in_specs=[pl.BlockSpec((tm, K), lambda i, j: (i, 0)),
                  pl.BlockSpec((K, tn), lambda i, j: (0, j)),
                  pl.BlockSpec((tm, tn), lambda i, j: (i, j))],
        out_specs=pl.BlockSpec((tm, tn), lambda i, j: (i, j)),
        out_shape=jax.ShapeDtypeStruct((T, N), F32),
        compiler_params=_params(("parallel", "arbitrary")),
        name="matmul_residual",
    )(a, w, res)


def _rmsnorm_kernel(x_ref, g_ref, o_ref):
    x = x_ref[...]
    var = jnp.mean(x * x, axis=-1, keepdims=True)
    o_ref[...] = x * lax.rsqrt(var + NORM_EPS) * g_ref[...]


def rmsnorm_rows(x, g, tm):
    T, D = x.shape
    tm = min(tm, T)
    return pl.pallas_call(
        _rmsnorm_kernel,
        grid=(T // tm,),
        in_specs=[pl.BlockSpec((tm, D), lambda i: (i, 0)),
                  pl.BlockSpec((1, D), lambda i: (0, 0))],
        out_specs=pl.BlockSpec((tm, D), lambda i: (i, 0)),
        out_shape=jax.ShapeDtypeStruct((T, D), F32),
        compiler_params=_params(("parallel",)),
        name="final_rmsnorm",
    )(x, g)


GDN_CHUNK = 128
GDN_HEADS_PER_STEP = 2
GDN_PREP_GROUP = 4


def _gdn_kernel(q_ref, k_ref, v_ref, gate_ref, ab_ref, cwq_ref, cwk_ref, cwv_ref,
                alog_ref, dtb_ref, gn_ref, o_ref,
                qn, kn, vn, gb, bb, us, ws, qds, kdts, ints, egl):
    C = GDN_CHUNK
    hb = GDN_HEADS_PER_STEP
    cg = GDN_PREP_GROUP
    S = q_ref.shape[1]
    n_chunks = S // C
    n_groups = n_chunks // cg
    head0 = pl.program_id(1) * hb
    scale = HEAD_DIM ** -0.5
    heads = [slice(hh * HEAD_DIM, (hh + 1) * HEAD_DIM) for hh in range(hb)]

    row_s = lax.broadcasted_iota(I32, (S, LANES), 0)
    lane_s = lax.broadcasted_iota(I32, (S, LANES), 1)

    def conv_silu(x, cw):
        x = x.astype(F32)
        y = x * cw[GDN_CONV - 1:GDN_CONV, :]
        for s in range(1, GDN_CONV):
            xs = jnp.where(row_s >= s, pltpu.roll(x, s, 0), 0.0)
            y = y + xs * cw[GDN_CONV - 1 - s:GDN_CONV - s, :]
        return y * _sigmoid(y)

    def l2n(x):
        return x * lax.rsqrt(jnp.sum(x * x, axis=-1, keepdims=True) + NORM_EPS)

    ab = ab_ref[0]
    g_all = -jnp.exp(alog_ref[...]) * _softplus(ab + dtb_ref[...])
    b_all = _sigmoid(ab)
    for hh, hs in enumerate(heads):
        qn[hh] = l2n(conv_silu(q_ref[0, :, hs], cwq_ref[:, hs]))
        kn[hh] = l2n(conv_silu(k_ref[0, :, hs], cwk_ref[:, hs]))
        vn[hh] = conv_silu(v_ref[0, :, hs], cwv_ref[:, hs])
        g_col = jnp.sum(jnp.where(lane_s == head0 + hh, g_all, 0.0), axis=1, keepdims=True)
        b_col = jnp.sum(jnp.where(lane_s == head0 + hh + GDN_HEADS, b_all, 0.0), axis=1, keepdims=True)
        gb[hh] = jnp.broadcast_to(g_col, (S, LANES))
        bb[hh] = jnp.broadcast_to(b_col, (S, LANES))

    row = lax.broadcasted_iota(I32, (C, C), 0)
    col = lax.broadcasted_iota(I32, (C, C), 1)
    eye = (row == col).astype(F32)
    n_doublings = int(np.log2(C)) - 1

    def prep(i, carry):
        hh = i // n_groups
        c0 = (i % n_groups) * cg
        chunks = range(cg)
        sls = [pl.ds(pl.multiple_of((c0 + j) * C, C), C) for j in chunks]
        gcs = []
        for sl in sls:
            gc = gb[hh, sl, :]
            s = 1
            while s < C:
                gc = gc + jnp.where(row >= s, pltpu.roll(gc, s, 0), 0.0)
                s *= 2
            gcs.append(gc)
        decays = [jnp.exp(jnp.where(row >= col, gc - gc.T, NEG_INF)) for gc in gcs]
        kcs = [kn[hh, sl, :] for sl in sls]
        qcs = [qn[hh, sl, :] * scale for sl in sls]
        bcs = [bb[hh, sl, :] for sl in sls]
        kbs = [kcs[j] * bcs[j] for j in chunks]
        kc16 = [kc.astype(BF16) for kc in kcs]
        kk = [_dot_nt(kbs[j].astype(BF16), kc16[j]) for j in chunks]
        qk = [_dot_nt(qcs[j].astype(BF16), kc16[j]) for j in chunks]
        lowers = [jnp.where(row > col, kk[j] * decays[j], 0.0) for j in chunks]
        for j in chunks:
            ints[hh, sls[j], :] = jnp.where(row >= col, qk[j] * decays[j], 0.0).astype(BF16)
        ps = [_dot_split(l, l) for l in lowers]
        ts = [eye - l for l in lowers]
        for d in range(n_doublings):
            if d < n_doublings - 1:
                ys = [_dot_split(jnp.concatenate([ts[j], ps[j]], axis=0), ps[j]) for j in chunks]
                ts = [ts[j] + ys[j][:C] for j in chunks]
                ps = [ys[j][C:] for j in chunks]
            else:
                ys = [_dot_split(ts[j], ps[j]) for j in chunks]
                ts = [ts[j] + ys[j] for j in chunks]
        egcs = [jnp.exp(gc) for gc in gcs]
        rhs = [jnp.concatenate([vn[hh, sls[j], :] * bcs[j], kbs[j] * egcs[j]], axis=1) for j in chunks]
        sol = [_dot_split(ts[j], rhs[j]) for j in chunks]
        for j in chunks:
            sl = sls[j]
            us[hh, sl, :] = sol[j][:, :HEAD_DIM]
            ws[hh, sl, :] = sol[j][:, HEAD_DIM:].astype(BF16)
            qds[hh, sl, :] = (qcs[j] * egcs[j]).astype(BF16)
            g_last = gcs[j][C - 1:C, :]
            kdts[hh, sl, :] = (kcs[j] * jnp.exp(g_last - gcs[j])).T.astype(BF16)
            egl[hh, pl.ds(pl.multiple_of((c0 + j) * 8, 8), 8), :] = jnp.broadcast_to(jnp.exp(g_last), (8, LANES))
        return carry

    lax.fori_loop(0, hb * n_groups, prep, 0)

    gn = gn_ref[...]
    hrange = range(hb)

    def step(c, states):
        sl = pl.ds(pl.multiple_of(c * C, C), C)
        s16 = [st.astype(BF16) for st in states]
        w_s = [_dot(ws[hh, sl, :], s16[hh]) for hh in hrange]
        q_s = [_dot(qds[hh, sl, :], s16[hh]) for hh in hrange]
        vn16 = [(us[hh, sl, :] - w_s[hh]).astype(BF16) for hh in hrange]
        o_in = [_dot(ints[hh, sl, :], vn16[hh]) for hh in hrange]
        kv = [_dot(kdts[hh, sl, :], vn16[hh]) for hh in hrange]
        new_states = []
        for hh in hrange:
            eg = egl[hh, pl.ds(pl.multiple_of(c * 8, 8), 8), :][0:1, :]
            new_states.append(states[hh] * eg + kv[hh])
            o = q_s[hh] + o_in[hh]
            var = jnp.mean(o * o, axis=-1, keepdims=True)
            gate = gate_ref[0, sl, heads[hh]].astype(F32)
            o_ref[0, sl, heads[hh]] = (o * lax.rsqrt(var + NORM_EPS) * gn * (gate * _sigmoid(gate))).astype(o_ref.dtype)
        return tuple(new_states)

    lax.fori_loop(0, n_chunks, step, tuple(jnp.zeros((HEAD_DIM, HEAD_DIM), F32) for _ in hrange))


def gdn_mixer(proj, ab, conv_w, a_log, dt_bias, gdn_norm):
    B, S, _ = proj.shape
    hb = GDN_HEADS_PER_STEP
    n_grp = GDN_HEADS // hb
    w = hb * HEAD_DIM
    blk = lambda part: pl.BlockSpec((1, S, w), lambda b, g: (b, 0, part * n_grp + g))
    cw = lambda part: pl.BlockSpec((GDN_CONV, w), lambda b, g: (0, part * n_grp + g))
    vec = pl.BlockSpec((1, LANES), lambda b, g: (0, 0))
    seq_f32 = pltpu.VMEM((hb, S, HEAD_DIM), F32)
    seq_b16 = pltpu.VMEM((hb, S, HEAD_DIM), BF16)
    return pl.pallas_call(
        _gdn_kernel,
        grid=(B, n_grp),
        in_specs=[blk(0), blk(1), blk(2), blk(3),
                  pl.BlockSpec((1, S, LANES), lambda b, g: (b, 0, 0)),
                  cw(0), cw(1), cw(2), vec, vec, vec],
        out_specs=pl.BlockSpec((1, S, w), lambda b, g: (b, 0, g)),
        out_shape=jax.ShapeDtypeStruct((B, S, GDN_HEADS * HEAD_DIM), BF16),
        scratch_shapes=[seq_f32, seq_f32, seq_f32, seq_f32, seq_f32, seq_f32,
                        seq_b16, seq_b16, seq_b16, seq_b16,
                        pltpu.VMEM((hb, 8 * (S // GDN_CHUNK), LANES), F32)],
        compiler_params=_params(("parallel", "arbitrary")),
        name="gdn_mixer",
    )(proj, proj, proj, proj, ab, conv_w, conv_w, conv_w, a_log, dt_bias, gdn_norm)


SB_TILE = 128
SB_HEADS_PER_STEP = 8
SB_DEAD = -104.0


def _sb_kernel(q_ref, k_ref, v_ref, o_ref, carry_s, acc_s):
    t = SB_TILE
    hb = SB_HEADS_PER_STEP
    qi = pl.program_id(1)
    scale = HEAD_DIM ** -0.5
    row2 = lax.broadcasted_iota(I32, (t, 2 * t), 0)
    col2 = lax.broadcasted_iota(I32, (t, 2 * t), 1)
    after_ones = jnp.where(col2 >= t, 1.0, jnp.where(row2 > col2, 1.0, 0.0)).astype(BF16)
    after_ones2 = jnp.concatenate([after_ones, after_ones], axis=0)
    causal = lax.broadcasted_iota(I32, (t, t), 1) < lax.broadcasted_iota(I32, (t, t), 0)

    heads = [slice(hh * HEAD_DIM, (hh + 1) * HEAD_DIM) for hh in range(hb)]

    def blocks(s0, diag):
        zs = [_dot_nt(q_ref[0, :, hs], k_ref[0, pl.ds(s0, t), hs]) * scale for hs in heads]
        rs, es = [], []
        for z in zs:
            ls = -(jnp.maximum(z, 0.0) + jnp.log(1.0 + jnp.exp(-jnp.abs(z))))
            if diag:
                ls = jnp.where(causal, ls, 0.0)
            hi, lo = _split(ls)
            rs.append(_dot(jnp.concatenate([hi, lo], axis=1), after_ones2))
            es.append(z + ls)
        pvs = []
        for hh, hs in enumerate(heads):
            between = rs[hh][:, :t] if diag else rs[hh][:, :t] + carry_s[hh]
            a = jnp.exp(es[hh] + between)
            if diag:
                a = jnp.where(causal, a, 0.0)
            pvs.append(_dot(a.astype(BF16), v_ref[0, pl.ds(s0, t), hs]))
        for hh in range(hb):
            if diag:
                acc_s[hh] = pvs[hh]
                carry_s[hh] = rs[hh][:, t:]
            else:
                acc_s[hh] += pvs[hh]
                carry_s[hh] += rs[hh][:, t:]

    def any_live():
        m = carry_s[0]
        for hh in range(1, hb):
            m = jnp.maximum(m, carry_s[hh])
        return jnp.max(m) >= SB_DEAD

    blocks(pl.multiple_of(qi * t, t), True)

    def cond(st):
        return (st[0] < qi) & st[1]

    def body(st):
        j = qi - 1 - st[0]
        blocks(pl.multiple_of(j * t, t), False)
        return st[0] + 1, any_live()

    lax.while_loop(cond, body, (jnp.int32(0), any_live()))
    for hh in range(hb):
        o_ref[0, :, hh * HEAD_DIM:(hh + 1) * HEAD_DIM] = acc_s[hh].astype(o_ref.dtype)


def sb_attention(proj, col0):
    B, S, _ = proj.shape
    hb = SB_HEADS_PER_STEP
    n_grp = SB_HEADS // hb
    t = SB_TILE
    w = hb * HEAD_DIM
    c0 = col0 // hb
    return pl.pallas_call(
        _sb_kernel,
        grid=(B * n_grp, S // t),
        in_specs=[pl.BlockSpec((1, t, w), lambda g, i: (g // n_grp, i, c0 + g % n_grp)),
                  pl.BlockSpec((1, S, w), lambda g, i: (g // n_grp, 0, c0 + n_grp + g % n_grp)),
                  pl.BlockSpec((1, S, w), lambda g, i: (g // n_grp, 0, c0 + 2 * n_grp + g % n_grp))],
        out_specs=pl.BlockSpec((1, t, w), lambda g, i: (g // n_grp, i, g % n_grp)),
        out_shape=jax.ShapeDtypeStruct((B, S, SB_HEADS * HEAD_DIM), BF16),
        scratch_shapes=[pltpu.VMEM((hb, t, t), F32), pltpu.VMEM((hb, t, HEAD_DIM), F32)],
        compiler_params=_params(("parallel", "arbitrary")),
        name="sb_attention",
    )(proj, proj, proj)


ROPE_TM = 512


def _rope_kernel(main_ref, small_ref, pos_ref, f_attn_ref, f_idx_ref,
                 q_ref, k_ref, vt_ref, qi_ref, kilo_ref, kihi_ref, wi_ref):
    tm = main_ref.shape[0]
    pos = pos_ref[...]
    lane = lax.broadcasted_iota(I32, (tm, LANES), 1)

    def tables(f_ref, period, half):
        ang = pos * f_ref[...]
        c = jnp.cos(ang)
        s = jnp.sin(ang)
        p = lane % period
        return c, jnp.where(p >= half, s, 0.0), jnp.where(p < half, -s, 0.0)

    def rot(x, tab, half):
        c, s_up, s_dn = tab
        return x * c + pltpu.roll(x, half, 1) * s_up + pltpu.roll(x, LANES - half, 1) * s_dn

    half_a = HEAD_DIM // ROPE_FRACTION // 2
    half_i = IDX_DIM // ROPE_FRACTION // 2
    tab_a = tables(f_attn_ref, HEAD_DIM, half_a)
    tab_i = tables(f_idx_ref, IDX_DIM, half_i)

    nq = ATTN_HEADS
    nk = ATTN_KV_HEADS
    for c in range(nq):
        x = main_ref[:, c * LANES:(c + 1) * LANES].astype(F32)
        q_ref[:, c * LANES:(c + 1) * LANES] = rot(x, tab_a, half_a).astype(BF16)
    for c in range(nk):
        x = main_ref[:, (nq + c) * LANES:(nq + c + 1) * LANES].astype(F32)
        k_ref[:, c * LANES:(c + 1) * LANES] = rot(x, tab_a, half_a).astype(BF16)
    vt_ref[0] = main_ref[:, (nq + nk) * LANES:(nq + 2 * nk) * LANES].astype(F32).T.astype(BF16)
    c0 = nq + 2 * nk
    for c in range(IDX_HEADS * IDX_DIM // LANES):
        x = main_ref[:, (c0 + c) * LANES:(c0 + c + 1) * LANES].astype(F32)
        qi_ref[:, c * LANES:(c + 1) * LANES] = rot(x, tab_i, half_i).astype(BF16)
    sm = small_ref[...]
    ki = jnp.where(lane < IDX_DIM, rot(sm, tab_i, half_i), 0.0)
    kilo_ref[...] = ki.astype(BF16)
    kihi_ref[...] = pltpu.roll(ki, IDX_DIM, 1).astype(BF16)
    wi_scale = (IDX_HEADS ** -0.5) * (IDX_DIM ** -0.5)
    wi_ref[...] = jnp.where(lane < IDX_HEADS, pltpu.roll(sm, LANES - IDX_DIM, 1) * wi_scale, 0.0)


def rope_split(main, small, pos_b, f_attn, f_idx, seq_len):
    T = main.shape[0]
    tm = ROPE_TM
    per_seq = seq_len // tm
    row = lambda n: pl.BlockSpec((tm, n), lambda i: (i, 0))
    vec = pl.BlockSpec((1, LANES), lambda i: (0, 0))
    nq, nk = ATTN_HEADS * HEAD_DIM, ATTN_KV_HEADS * HEAD_DIM
    ni = IDX_HEADS * IDX_DIM
    return pl.pallas_call(
        _rope_kernel,
        grid=(T // tm,),
        in_specs=[row(main.shape[1]), row(LANES), row(LANES), vec, vec],
        out_specs=[row(nq), row(nk), pl.BlockSpec((1, nk, tm), lambda i: (i // per_seq, 0, i % per_seq)),
                   row(ni), row(LANES), row(LANES), row(LANES)],
        out_shape=[jax.ShapeDtypeStruct((T, nq), BF16), jax.ShapeDtypeStruct((T, nk), BF16),
                   jax.ShapeDtypeStruct((T // seq_len, nk, seq_len), BF16), jax.ShapeDtypeStruct((T, ni), BF16),
                   jax.ShapeDtypeStruct((T, LANES), BF16), jax.ShapeDtypeStruct((T, LANES), BF16),
                   jax.ShapeDtypeStruct((T, LANES), F32)],
        compiler_params=_params(("parallel",)),
        name="rope_split",
    )(main, small, pos_b, f_attn, f_idx)


DSA_TQ = 256
DSA_TK = 256
INT_MIN = -2 ** 31


def _dsa_kernel(q_ref, k_ref, vt_ref, qi_ref, kilo_ref, kihi_ref, wi_ref, o_ref,
                scores, m_s, l_s, acc_s):
    tq, tk = DSA_TQ, DSA_TK
    S = k_ref.shape[1]
    topk = min(TOPK_MAX, S // TOPK_KEY_FRACTION)
    q0 = pl.program_id(1) * tq
    n_live = (q0 + tq + tk - 1) // tk
    exp2_scale = (HEAD_DIM ** -0.5) * float(np.log2(np.e))

    key_i = lax.broadcasted_iota(I32, (tk, tq), 0)
    qry_i = q0 + lax.broadcasted_iota(I32, (tk, tq), 1)
    neg_key = INT_MIN + 0x7FFFFF

    def tile_start(kt):
        return pl.multiple_of(kt * tk, tk)

    def key_to_float(key):
        return pltpu.bitcast(jnp.where(key < 0, key ^ 0x7FFFFFFF, key), F32)

    wit = wi_ref[0].T
    qi_pairs = jnp.concatenate([qi_ref[0, :, c * LANES:(c + 1) * LANES]
                                for c in range(IDX_HEADS * IDX_DIM // LANES)], axis=0)

    def score_tile(kt, carry):
        s0 = tile_start(kt)
        even = _dot_nt(kilo_ref[0, pl.ds(s0, tk), :], qi_pairs)
        odd = _dot_nt(kihi_ref[0, pl.ds(s0, tk), :], qi_pairs)
        score = None
        for h in range(IDX_HEADS):
            logit = (even if h % 2 == 0 else odd)[:, (h // 2) * tq:(h // 2 + 1) * tq]
            term = wit[h:h + 1, :] * jnp.maximum(logit, 0.0)
            score = term if score is None else score + term
        scores[pl.ds(s0, tk), :] = jnp.where(s0 + key_i <= qry_i, score, NEG_INF)
        return carry

    lax.fori_loop(0, n_live, score_tile, 0)

    @pl.when(n_live % 2 == 1)
    def _():
        scores[pl.ds(tile_start(n_live), tk), :] = jnp.full((tk, tq), NEG_INF, F32)

    kf = float(topk)
    ts = 2 * tk
    n_sums = 4

    def count(above):
        def tile(kt, acc):
            hit = jnp.where(above(scores[pl.ds(pl.multiple_of(kt * ts, ts), ts), :]), 1.0, 0.0)
            return acc + jnp.sum(hit.reshape(n_sums, ts // (8 * n_sums), 8, tq), axis=1)
        acc = lax.fori_loop(0, (n_live + 1) // 2, tile, jnp.zeros((n_sums, 8, tq), F32))
        return jnp.sum(acc.reshape(n_sums * 8, tq), axis=0, keepdims=True)

    zero = jnp.zeros((1, tq), I32)
    prefix = jnp.where(count(lambda s: s >= 0.0) >= kf, zero, jnp.full((1, tq), INT_MIN, I32))

    def bit_step(i, prefix):
        cand = prefix | jnp.left_shift(jnp.int32(1), 30 - i)
        cand_f = key_to_float(cand)
        return jnp.where(count(lambda s: s >= cand_f) >= kf, cand, prefix)

    thr_key = lax.fori_loop(0, 31, bit_step, prefix)
    thr = jnp.where(thr_key < neg_key, NEG_INF, key_to_float(thr_key))
    need = kf - count(lambda s: s > thr)

    m_s[...] = jnp.full(m_s.shape, -1e30, F32)
    l_s[...] = jnp.zeros(l_s.shape, F32)
    acc_s[...] = jnp.zeros(acc_s.shape, F32)
    before = (lax.broadcasted_iota(I32, (tk, tk), 1) < lax.broadcasted_iota(I32, (tk, tk), 0)).astype(BF16)
    groups = range(ATTN_KV_HEADS)
    q4 = [jnp.concatenate([(q_ref[0, :, (g * GQA_GROUP + u) * HEAD_DIM:(g * GQA_GROUP + u + 1) * HEAD_DIM]
                            .astype(F32) * exp2_scale).astype(BF16)
                           for u in range(GQA_GROUP)], axis=0) for g in groups]
    ones_rows = jnp.ones((16, tk), BF16)

    def attn_tile(kt, ties_seen):
        s0 = tile_start(kt)
        score = scores[pl.ds(s0, tk), :]
        eq = score == thr
        rank = _dot(before, jnp.where(eq, 1.0, 0.0).astype(BF16)) + ties_seen
        sel = ((score > thr) | (eq & (rank < need))) & (s0 + key_i <= qry_i)
        bias = jnp.where(sel, 0.0, -1e30).astype(BF16)
        sts = [_dot_nt(k_ref[0, pl.ds(s0, tk), g * HEAD_DIM:(g + 1) * HEAD_DIM], q4[g]) for g in groups]
        ps, alphas = [], []
        for g in groups:
            p_g, a_g = [], []
            for u in range(GQA_GROUP):
                h = g * GQA_GROUP + u
                s = sts[g][:, u * tq:(u + 1) * tq].astype(BF16) + bias
                m_old = m_s[h]
                m_new = jnp.maximum(m_old, jnp.max(s, axis=0, keepdims=True).astype(F32))
                p_g.append(jnp.exp2(s - m_new.astype(BF16)))
                a_g.append(jnp.exp2(m_old - m_new))
                m_s[h] = m_new
            ps.append(jnp.concatenate(p_g, axis=1))
            alphas.append(jnp.concatenate(a_g, axis=1))
        pvs = [_dot(vt_ref[0, g * HEAD_DIM:(g + 1) * HEAD_DIM, pl.ds(s0, tk)], ps[g]) for g in groups]
        sums = [_dot(ones_rows, ps[g])[0:1, :] for g in groups]
        for g in groups:
            acc_s[g] = alphas[g] * acc_s[g] + pvs[g]
            l_s[g] = alphas[g] * l_s[g] + sums[g]
        return ties_seen + jnp.sum(jnp.where(eq, 1.0, 0.0), axis=0, keepdims=True)

    lax.fori_loop(0, n_live, attn_tile, jnp.zeros((1, tq), F32))

    for g in groups:
        for u in range(GQA_GROUP):
            h = g * GQA_GROUP + u
            cols = slice(u * tq, (u + 1) * tq)
            o_t = acc_s[g, :, cols] / l_s[g, :, cols]
            o_ref[0, :, h * HEAD_DIM:(h + 1) * HEAD_DIM] = o_t.T.astype(o_ref.dtype)


def dsa_attention(q, k, vt, qi, ki_lo, ki_hi, wi):
    B, S, _ = q.shape
    tq = DSA_TQ
    qblk = lambda n: pl.BlockSpec((1, tq, n), lambda b, i: (b, i, 0))
    seq = lambda n: pl.BlockSpec((1, S, n), lambda b, i: (b, 0, 0))
    return pl.pallas_call(
        _dsa_kernel,
        grid=(B, S // tq),
        in_specs=[qblk(q.shape[2]), seq(k.shape[2]),
                  pl.BlockSpec((1, vt.shape[1], S), lambda b, i: (b, 0, 0)),
                  qblk(qi.shape[2]), seq(LANES), seq(LANES), qblk(LANES)],
        out_specs=qblk(q.shape[2]),
        out_shape=jax.ShapeDtypeStruct(q.shape, BF16),
        scratch_shapes=[pltpu.VMEM((S, tq), F32),
                        pltpu.VMEM((ATTN_HEADS, 1, tq), F32),
                        pltpu.VMEM((ATTN_KV_HEADS, 1, GQA_GROUP * tq), F32),
                        pltpu.VMEM((ATTN_KV_HEADS, HEAD_DIM, GQA_GROUP * tq), F32)],
        compiler_params=_params(("parallel", "arbitrary")),
        name="dsa_attention",
    )(q, k, vt, qi, ki_lo, ki_hi, wi)


MM_TM = 1024
MM_TN = 512


def _pad_cols(w, n):
    return jnp.pad(w, ((0, 0), (0, n - w.shape[1])))


def _row(v):
    return v.reshape(1, -1).astype(F32)


def _lane_vec(v):
    return jnp.pad(v.astype(F32), (0, LANES - v.shape[0])).reshape(1, LANES)


def _rope_freqs(width, period):
    half = width // ROPE_FRACTION // 2
    inv_freq = ROPE_THETA ** (-jnp.arange(half, dtype=F32) / half)
    head = jnp.concatenate([inv_freq, inv_freq, jnp.zeros((period - 2 * half,), F32)])
    return jnp.tile(head, LANES // period).reshape(1, LANES)


def _ffn(x, g, w_gate, w_up, w_down):
    u = ffn_gate_up(x, _row(g), w_gate.astype(BF16), w_up.astype(BF16), MM_TM, MM_TN)
    return matmul_residual(u, w_down.astype(BF16), x, MM_TM, MM_TN)


def _mixer_ab(xf, B, S, norm_g, w_in, conv_w, a_log, dt_bias, gdn_norm, w_out):
    T = B * S
    gw = GDN_HEADS * HEAD_DIM
    sw = SB_HEADS * HEAD_DIM
    n_small0 = 4 * gw
    w_main = jnp.concatenate([w_in[:, :n_small0], w_in[:, n_small0 + 2 * GDN_HEADS:]], axis=1).astype(BF16)
    w_small = _pad_cols(w_in[:, n_small0:n_small0 + 2 * GDN_HEADS], LANES).astype(BF16)
    g0 = _row(norm_g)
    proj, ab = rms_matmul(xf, g0, w_main, w_small, MM_TM, MM_TN)
    proj = proj.reshape(B, S, -1)
    ab = ab.reshape(B, S, LANES)
    o_a = gdn_mixer(proj, ab, conv_w, _lane_vec(a_log), _lane_vec(dt_bias), _row(gdn_norm))
    o_b = sb_attention(proj, (4 * gw) // HEAD_DIM)
    o = jnp.concatenate([o_a, o_b], axis=-1).reshape(T, gw + sw)
    return matmul_residual(o, w_out.astype(BF16), xf, MM_TM, MM_TN)


def _mixer_c(xf, B, S, positions, norm_g, w_in, w_out):
    T = B * S
    n_main = (ATTN_HEADS + 2 * ATTN_KV_HEADS) * HEAD_DIM + IDX_HEADS * IDX_DIM
    g1 = _row(norm_g)
    main, small = rms_matmul(xf, g1, w_in[:, :n_main].astype(BF16),
                             _pad_cols(w_in[:, n_main:], LANES).astype(BF16), MM_TM, MM_TN)
    pos_b = jnp.broadcast_to(positions.reshape(T, 1).astype(F32), (T, LANES))
    q, k, vt, qi, ki_lo, ki_hi, wi = rope_split(main, small, pos_b, _rope_freqs(HEAD_DIM, HEAD_DIM),
                                                _rope_freqs(IDX_DIM, IDX_DIM), S)
    r3 = lambda t: t.reshape(B, S, -1)
    o = dsa_attention(r3(q), r3(k), vt, r3(qi), r3(ki_lo), r3(ki_hi), r3(wi)).reshape(T, -1)
    return matmul_residual(o, w_out.astype(BF16), xf, MM_TM, MM_TN)


def kernel(x, positions, norm_mix, norm_ffn, final_norm, w_in_ab, conv_w_a, a_log, dt_bias, gdn_norm,
           w_out_ab, w_in_c, w_out_c, ffn_gate, ffn_up, ffn_down):
    B, S, D = x.shape
    xf = x.reshape(B * S, D)
    xf = _mixer_ab(xf, B, S, norm_mix[0], w_in_ab[0], conv_w_a[0], a_log[0], dt_bias[0], gdn_norm[0], w_out_ab[0])
    xf = _ffn(xf, norm_ffn[0], ffn_gate[0], ffn_up[0], ffn_down[0])
    xf = _mixer_c(xf, B, S, positions, norm_mix[1], w_in_c[0], w_out_c[0])
    xf = _ffn(xf, norm_ffn[1], ffn_gate[1], ffn_up[1], ffn_down[1])
    return rmsnorm_rows(xf, _row(final_norm), MM_TM).reshape(B, S, D)
```

```python
import functools

import numpy as np
import jax
import jax.numpy as jnp
from jax import lax
from jax.experimental import pallas as pl
from jax.experimental.pallas import tpu as pltpu

F32 = jnp.float32
BF16 = jnp.bfloat16
I32 = jnp.int32

HEAD_DIM = 128
GDN_HEADS = 8
SB_HEADS = 8
GDN_CONV = 4
ATTN_HEADS = 16
ATTN_KV_HEADS = 4
GQA_GROUP = ATTN_HEADS // ATTN_KV_HEADS
IDX_HEADS = 16
IDX_DIM = 64
TOPK_MAX = 256
TOPK_KEY_FRACTION = 4
ROPE_THETA = 500000.0
ROPE_FRACTION = 4
NORM_EPS = 1e-6

LANES = 128
VMEM_LIMIT = 56 * 1024 * 1024
NEG_INF = float("-inf")


def _params(sem):
    return pltpu.CompilerParams(dimension_semantics=sem, vmem_limit_bytes=VMEM_LIMIT)


def _dot(a, b):
    return jnp.dot(a, b, preferred_element_type=F32)


def _dot_nt(a, b):
    return lax.dot_general(a, b, (((1,), (1,)), ((), ())), preferred_element_type=F32)


def _split(a):
    hi = a.astype(BF16)
    lo = (a - hi.astype(F32)).astype(BF16)
    return hi, lo


def _dot_split(a, b):
    ah, al = _split(a)
    bh, bl = _split(b)
    return _dot(jnp.concatenate([ah, al], axis=1), jnp.concatenate([bh, bh], axis=0)) + _dot(ah, bl)


def _softplus(x):
    return jnp.maximum(x, 0.0) + jnp.log1p(jnp.exp(-jnp.abs(x)))


def _sigmoid(x):
    return 1.0 / (1.0 + jnp.exp(-x))


def _rms_matmul_kernel(x_ref, g_ref, w_ref, ws_ref, o_ref, os_ref, h_ref):
    @pl.when(pl.program_id(1) == 0)
    def _():
        x = x_ref[...]
        var = jnp.mean(x * x, axis=-1, keepdims=True)
        h_ref[...] = (x * lax.rsqrt(var + NORM_EPS) * g_ref[...]).astype(BF16)
        os_ref[...] = _dot(h_ref[...], ws_ref[...])

    o_ref[...] = _dot(h_ref[...], w_ref[...].astype(BF16)).astype(o_ref.dtype)


def rms_matmul(x, g, w, n_cols, w_small, tm, tn):
    T, D = x.shape
    tm = min(tm, T)
    N = n_cols
    return pl.pallas_call(
        _rms_matmul_kernel,
        grid=(T // tm, N // tn),
        in_specs=[pl.BlockSpec((tm, D), lambda i, j: (i, 0)),
                  pl.BlockSpec((1, D), lambda i, j: (0, 0)),
                  pl.BlockSpec((D, tn), lambda i, j: (0, j)),
                  pl.BlockSpec((D, LANES), lambda i, j: (0, 0))],
        out_specs=[pl.BlockSpec((tm, tn), lambda i, j: (i, j)),
                   pl.BlockSpec((tm, LANES), lambda i, j: (i, 0))],
        out_shape=[jax.ShapeDtypeStruct((T, N), BF16), jax.ShapeDtypeStruct((T, LANES), F32)],
        scratch_shapes=[pltpu.VMEM((tm, D), BF16)],
        compiler_params=_params(("parallel", "arbitrary")),
        name="rms_matmul",
    )(x, g, w, w_small)


def _ffn_up_kernel(x_ref, g_ref, wg_ref, wu_ref, o_ref, h_ref):
    @pl.when(pl.program_id(1) == 0)
    def _():
        x = x_ref[...]
        var = jnp.mean(x * x, axis=-1, keepdims=True)
        h_ref[...] = (x * lax.rsqrt(var + NORM_EPS) * g_ref[...]).astype(BF16)

    h = h_ref[...]
    a = _dot(h, wg_ref[...].astype(BF16))
    b = _dot(h, wu_ref[...].astype(BF16))
    o_ref[...] = (a * _sigmoid(a) * b).astype(o_ref.dtype)


def ffn_gate_up(x, g, wg, wu, tm, tn):
    T, D = x.shape
    tm = min(tm, T)
    N = wg.shape[1]
    return pl.pallas_call(
        _ffn_up_kernel,
        grid=(T // tm, N // tn),
        in_specs=[pl.BlockSpec((tm, D), lambda i, j: (i, 0)),
                  pl.BlockSpec((1, D), lambda i, j: (0, 0)),
                  pl.BlockSpec((D, tn), lambda i, j: (0, j)),
                  pl.BlockSpec((D, tn), lambda i, j: (0, j))],
        out_specs=pl.BlockSpec((tm, tn), lambda i, j: (i, j)),
        out_shape=jax.ShapeDtypeStruct((T, N), BF16),
        scratch_shapes=[pltpu.VMEM((tm, D), BF16)],
        compiler_params=_params(("parallel", "arbitrary")),
        name="ffn_up",
    )(x, g, wg, wu)


def _matmul_res_kernel(a_ref, w_ref, r_ref, o_ref):
    o_ref[...] = r_ref[...] + _dot(a_ref[...], w_ref[...])


def matmul_residual(a, w, res, tm, tn):
    T, K = a.shape
    tm = min(tm, T)
    N = w.shape[1]
    return pl.pallas_call(
        _matmul_res_kernel,
        grid=(T // tm, N // tn),
        in_specs=[pl.BlockSpec((tm, K), lambda i, j: (i, 0)),
                  pl.BlockSpec((K, tn), lambda i, j: (0, j)),
                  pl.BlockSpec((tm, tn), lambda i, j: (i, j))],
        out_specs=pl.BlockSpec((tm, tn), lambda i, j: (i, j)),
        out_shape=jax.ShapeDtypeStruct((T, N), F32),
        compiler_params=_params(("parallel", "arbitrary")),
        name="matmul_residual",
    )(a, w, res)


def _rmsnorm_kernel(x_ref, g_ref, o_ref):
    x = x_ref[...]
    var = jnp.mean(x * x, axis=-1, keepdims=True)
    o_ref[...] = x * lax.rsqrt(var + NORM_EPS) * g_ref[...]


def rmsnorm_rows(x, g, tm):
    T, D = x.shape
    tm = min(tm, T)
    return pl.pallas_call(
        _rmsnorm_kernel,
        grid=(T // tm,),
        in_specs=[pl.BlockSpec((tm, D), lambda i: (i, 0)),
                  pl.BlockSpec((1, D), lambda i: (0, 0))],
        out_specs=pl.BlockSpec((tm, D), lambda i: (i, 0)),
        out_shape=jax.ShapeDtypeStruct((T, D), F32),
        compiler_params=_params(("parallel",)),
        name="final_rmsnorm",
    )(x, g)


GDN_CHUNK = 128
GDN_HEADS_PER_STEP = 2
GDN_PREP_GROUP = 4


def _gdn_kernel(q_ref, k_ref, v_ref, gate_ref, ab_ref, cwq_ref, cwk_ref, cwv_ref,
                alog_ref, dtb_ref, gn_ref, o_ref,
                qn, kn, vn, gb, bb, us, ws, qds, kdts, ints, egl):
    C = GDN_CHUNK
    hb = GDN_HEADS_PER_STEP
    cg = GDN_PREP_GROUP
    S = q_ref.shape[1]
    n_chunks = S // C
    n_groups = n_chunks // cg
    head0 = pl.program_id(1) * hb
    scale = HEAD_DIM ** -0.5
    heads = [slice(hh * HEAD_DIM, (hh + 1) * HEAD_DIM) for hh in range(hb)]

    row_s = lax.broadcasted_iota(I32, (S, LANES), 0)
    lane_s = lax.broadcasted_iota(I32, (S, LANES), 1)

    def conv_silu(x, cw):
        x = x.astype(F32)
        y = x * cw[GDN_CONV - 1:GDN_CONV, :]
        for s in range(1, GDN_CONV):
            xs = jnp.where(row_s >= s, pltpu.roll(x, s, 0), 0.0)
            y = y + xs * cw[GDN_CONV - 1 - s:GDN_CONV - s, :]
        return y * _sigmoid(y)

    def l2n(x):
        return x * lax.rsqrt(jnp.sum(x * x, axis=-1, keepdims=True) + NORM_EPS)

    ab = ab_ref[0]
    g_all = -jnp.exp(alog_ref[...]) * _softplus(ab + dtb_ref[...])
    b_all = _sigmoid(ab)
    for hh, hs in enumerate(heads):
        qn[hh] = l2n(conv_silu(q_ref[0, :, hs], cwq_ref[:, hs]))
        kn[hh] = l2n(conv_silu(k_ref[0, :, hs], cwk_ref[:, hs]))
        vn[hh] = conv_silu(v_ref[0, :, hs], cwv_ref[:, hs])
        g_col = jnp.sum(jnp.where(lane_s == head0 + hh, g_all, 0.0), axis=1, keepdims=True)
        b_col = jnp.sum(jnp.where(lane_s == head0 + hh + GDN_HEADS, b_all, 0.0), axis=1, keepdims=True)
        gb[hh] = jnp.broadcast_to(g_col, (S, LANES))
        bb[hh] = jnp.broadcast_to(b_col, (S, LANES))

    row = lax.broadcasted_iota(I32, (C, C), 0)
    col = lax.broadcasted_iota(I32, (C, C), 1)
    eye = (row == col).astype(F32)
    n_doublings = int(np.log2(C)) - 1

    def prep(i, carry):
        hh = i // n_groups
        c0 = (i % n_groups) * cg
        chunks = range(cg)
        sls = [pl.ds(pl.multiple_of((c0 + j) * C, C), C) for j in chunks]
        gcs = []
        for sl in sls:
            gc = gb[hh, sl, :]
            s = 1
            while s < C:
                gc = gc + jnp.where(row >= s, pltpu.roll(gc, s, 0), 0.0)
                s *= 2
            gcs.append(gc)
        decays = [jnp.exp(jnp.where(row >= col, gc - gc.T, NEG_INF)) for gc in gcs]
        kcs = [kn[hh, sl, :] for sl in sls]
        qcs = [qn[hh, sl, :] * scale for sl in sls]
        bcs = [bb[hh, sl, :] for sl in sls]
        kbs = [kcs[j] * bcs[j] for j in chunks]
        kc16 = [kc.astype(BF16) for kc in kcs]
        kk = [_dot_nt(kbs[j].astype(BF16), kc16[j]) for j in chunks]
        qk = [_dot_nt(qcs[j].astype(BF16), kc16[j]) for j in chunks]
        lowers = [jnp.where(row > col, kk[j] * decays[j], 0.0) for j in chunks]
        for j in chunks:
            ints[hh, sls[j], :] = jnp.where(row >= col, qk[j] * decays[j], 0.0).astype(BF16)
        ps = [_dot_split(l, l) for l in lowers]
        ts = [eye - l for l in lowers]
        for d in range(n_doublings):
            if d < n_doublings - 1:
                ys = [_dot_split(jnp.concatenate([ts[j], ps[j]], axis=0), ps[j]) for j in chunks]
                ts = [ts[j] + ys[j][:C] for j in chunks]
                ps = [ys[j][C:] for j in chunks]
            else:
                ys = [_dot_split(ts[j], ps[j]) for j in chunks]
                ts = [ts[j] + ys[j] for j in chunks]
        egcs = [jnp.exp(gc) for gc in gcs]
        rhs = [jnp.concatenate([vn[hh, sls[j], :] * bcs[j], kbs[j] * egcs[j]], axis=1) for j in chunks]
        sol = [_dot_split(ts[j], rhs[j]) for j in chunks]
        for j in chunks:
            sl = sls[j]
            us[hh, sl, :] = sol[j][:, :HEAD_DIM]
            ws[hh, sl, :] = sol[j][:, HEAD_DIM:].astype(BF16)
            qds[hh, sl, :] = (qcs[j] * egcs[j]).astype(BF16)
            g_last = gcs[j][C - 1:C, :]
            kdts[hh, sl, :] = (kcs[j] * jnp.exp(g_last - gcs[j])).T.astype(BF16)
            egl[hh, pl.ds(pl.multiple_of((c0 + j) * 8, 8), 8), :] = jnp.broadcast_to(jnp.exp(g_last), (8, LANES))
        return carry

    lax.fori_loop(0, hb * n_groups, prep, 0)

    gn = gn_ref[...]
    hrange = range(hb)

    def step(c, states):
        sl = pl.ds(pl.multiple_of(c * C, C), C)
        s16 = [st.astype(BF16) for st in states]
        w_s = [_dot(ws[hh, sl, :], s16[hh]) for hh in hrange]
        q_s = [_dot(qds[hh, sl, :], s16[hh]) for hh in hrange]
        vn16 = [(us[hh, sl, :] - w_s[hh]).astype(BF16) for hh in hrange]
        o_in = [_dot(ints[hh, sl, :], vn16[hh]) for hh in hrange]
        kv = [_dot(kdts[hh, sl, :], vn16[hh]) for hh in hrange]
        new_states = []
        for hh in hrange:
            eg = egl[hh, pl.ds(pl.multiple_of(c * 8, 8), 8), :][0:1, :]
            new_states.append(states[hh] * eg + kv[hh])
            o = q_s[hh] + o_in[hh]
            var = jnp.mean(o * o, axis=-1, keepdims=True)
            gate = gate_ref[0, sl, heads[hh]].astype(F32)
            o_ref[0, sl, heads[hh]] = (o * lax.rsqrt(var + NORM_EPS) * gn * (gate * _sigmoid(gate))).astype(o_ref.dtype)
        return tuple(new_states)

    lax.fori_loop(0, n_chunks, step, tuple(jnp.zeros((HEAD_DIM, HEAD_DIM), F32) for _ in hrange))


def gdn_mixer(proj, ab, conv_w, a_log, dt_bias, gdn_norm):
    B, S, _ = proj.shape
    hb = GDN_HEADS_PER_STEP
    n_grp = GDN_HEADS // hb
    w = hb * HEAD_DIM
    blk = lambda part: pl.BlockSpec((1, S, w), lambda b, g: (b, 0, part * n_grp + g))
    cw = lambda part: pl.BlockSpec((GDN_CONV, w), lambda b, g: (0, part * n_grp + g))
    vec = pl.BlockSpec((1, LANES), lambda b, g: (0, 0))
    seq_f32 = pltpu.VMEM((hb, S, HEAD_DIM), F32)
    seq_b16 = pltpu.VMEM((hb, S, HEAD_DIM), BF16)
    return pl.pallas_call(
        _gdn_kernel,
        grid=(B, n_grp),
        in_specs=[blk(0), blk(1), blk(2), blk(3),
                  pl.BlockSpec((1, S, LANES), lambda b, g: (b, 0, 0)),
                  cw(0), cw(1), cw(2), vec, vec, vec],
        out_specs=pl.BlockSpec((1, S, w), lambda b, g: (b, 0, g)),
        out_shape=jax.ShapeDtypeStruct((B, S, GDN_HEADS * HEAD_DIM), BF16),
        scratch_shapes=[seq_f32, seq_f32, seq_f32, seq_f32, seq_f32, seq_f32,
                        seq_b16, seq_b16, seq_b16, seq_b16,
                        pltpu.VMEM((hb, 8 * (S // GDN_CHUNK), LANES), F32)],
        compiler_params=_params(("parallel", "arbitrary")),
        name="gdn_mixer",
    )(proj, proj, proj, proj, ab, conv_w, conv_w, conv_w, a_log, dt_bias, gdn_norm)


SB_TILE = 128
SB_HEADS_PER_STEP = 8
SB_DEAD = -104.0


def _sb_kernel(q_ref, k_ref, v_ref, o_ref, carry_s, acc_s):
    t = SB_TILE
    hb = SB_HEADS_PER_STEP
    qi = pl.program_id(1)
    scale = HEAD_DIM ** -0.5
    row2 = lax.broadcasted_iota(I32, (t, 2 * t), 0)
    col2 = lax.broadcasted_iota(I32, (t, 2 * t), 1)
    after_ones = jnp.where(col2 >= t, 1.0, jnp.where(row2 > col2, 1.0, 0.0)).astype(BF16)
    after_ones2 = jnp.concatenate([after_ones, after_ones], axis=0)
    causal = lax.broadcasted_iota(I32, (t, t), 1) < lax.broadcasted_iota(I32, (t, t), 0)

    heads = [slice(hh * HEAD_DIM, (hh + 1) * HEAD_DIM) for hh in range(hb)]

    def blocks(s0, diag):
        zs = [_dot_nt(q_ref[0, :, hs], k_ref[0, pl.ds(s0, t), hs]) * scale for hs in heads]
        rs, es = [], []
        for z in zs:
            ls = -(jnp.maximum(z, 0.0) + jnp.log(1.0 + jnp.exp(-jnp.abs(z))))
            if diag:
                ls = jnp.where(causal, ls, 0.0)
            hi, lo = _split(ls)
            rs.append(_dot(jnp.concatenate([hi, lo], axis=1), after_ones2))
            es.append(z + ls)
        pvs = []
        for hh, hs in enumerate(heads):
            between = rs[hh][:, :t] if diag else rs[hh][:, :t] + carry_s[hh]
            a = jnp.exp(es[hh] + between)
            if diag:
                a = jnp.where(causal, a, 0.0)
            pvs.append(_dot(a.astype(BF16), v_ref[0, pl.ds(s0, t), hs]))
        for hh in range(hb):
            if diag:
                acc_s[hh] = pvs[hh]
                carry_s[hh] = rs[hh][:, t:]
            else:
                acc_s[hh] += pvs[hh]
                carry_s[hh] += rs[hh][:, t:]

    def any_live():
        m = carry_s[0]
        for hh in range(1, hb):
            m = jnp.maximum(m, carry_s[hh])
        return jnp.max(m) >= SB_DEAD

    blocks(pl.multiple_of(qi * t, t), True)

    def cond(st):
        return (st[0] < qi) & st[1]

    def body(st):
        j = qi - 1 - st[0]
        blocks(pl.multiple_of(j * t, t), False)
        return st[0] + 1, any_live()

    lax.while_loop(cond, body, (jnp.int32(0), any_live()))
    for hh in range(hb):
        o_ref[0, :, hh * HEAD_DIM:(hh + 1) * HEAD_DIM] = acc_s[hh].astype(o_ref.dtype)


def sb_attention(proj, col0):
    B, S, _ = proj.shape
    hb = SB_HEADS_PER_STEP
    n_grp = SB_HEADS // hb
    t = SB_TILE
    w = hb * HEAD_DIM
    c0 = col0 // hb
    return pl.pallas_call(
        _sb_kernel,
        grid=(B * n_grp, S // t),
        in_specs=[pl.BlockSpec((1, t, w), lambda g, i: (g // n_grp, i, c0 + g % n_grp)),
                  pl.BlockSpec((1, S, w), lambda g, i: (g // n_grp, 0, c0 + n_grp + g % n_grp)),
                  pl.BlockSpec((1, S, w), lambda g, i: (g // n_grp, 0, c0 + 2 * n_grp + g % n_grp))],
        out_specs=pl.BlockSpec((1, t, w), lambda g, i: (g // n_grp, i, g % n_grp)),
        out_shape=jax.ShapeDtypeStruct((B, S, SB_HEADS * HEAD_DIM), BF16),
        scratch_shapes=[pltpu.VMEM((hb, t, t), F32), pltpu.VMEM((hb, t, HEAD_DIM), F32)],
        compiler_params=_params(("parallel", "arbitrary")),
        name="sb_attention",
    )(proj, proj, proj)


ROPE_TM = 512


def _rope_kernel(main_ref, small_ref, pos_ref, freq_ref,
                 q_ref, k_ref, vt_ref, qi_ref, kilo_ref, kihi_ref, wi_ref):
    tm = main_ref.shape[0]
    lane = lax.broadcasted_iota(I32, (tm, LANES), 1)
    half_a = HEAD_DIM // ROPE_FRACTION // 2
    half_i = IDX_DIM // ROPE_FRACTION // 2

    ang = pos_ref[...] * freq_ref[...]
    cos_p = jnp.cos(ang)
    sin_p = jnp.sin(ang)

    def table(c, s, period, half):
        p = lane % period
        return c, jnp.where(p >= half, s, 0.0), jnp.where(p < half, -s, 0.0)

    in_attn = lane < 2 * half_a
    tab_a = table(jnp.where(in_attn, cos_p, 1.0), jnp.where(in_attn, sin_p, 0.0), HEAD_DIM, half_a)

    first = lane < 2 * half_i
    second = (lane >= IDX_DIM) & (lane < IDX_DIM + 2 * half_i)

    def spread(v, fill):
        return jnp.where(first, pltpu.roll(v, LANES - 2 * half_a, 1),
                         jnp.where(second, pltpu.roll(v, IDX_DIM - 2 * half_a, 1), fill))

    tab_i = table(spread(cos_p, 1.0), spread(sin_p, 0.0), IDX_DIM, half_i)

    def rot(x, tab, half):
        c, s_up, s_dn = tab
        return x * c + pltpu.roll(x, half, 1) * s_up + pltpu.roll(x, LANES - half, 1) * s_dn

    nq = ATTN_HEADS
    nk = ATTN_KV_HEADS
    for c in range(nq):
        x = main_ref[:, c * LANES:(c + 1) * LANES].astype(F32)
        q_ref[:, c * LANES:(c + 1) * LANES] = rot(x, tab_a, half_a).astype(BF16)
    for c in range(nk):
        x = main_ref[:, (nq + c) * LANES:(nq + c + 1) * LANES].astype(F32)
        k_ref[:, c * LANES:(c + 1) * LANES] = rot(x, tab_a, half_a).astype(BF16)
    vt_ref[0] = main_ref[:, (nq + nk) * LANES:(nq + 2 * nk) * LANES].astype(F32).T.astype(BF16)
    c0 = nq + 2 * nk
    for c in range(IDX_HEADS * IDX_DIM // LANES):
        x = main_ref[:, (c0 + c) * LANES:(c0 + c + 1) * LANES].astype(F32)
        qi_ref[:, c * LANES:(c + 1) * LANES] = rot(x, tab_i, half_i).astype(BF16)
    sm = small_ref[...]
    ki = jnp.where(lane < IDX_DIM, rot(sm, tab_i, half_i), 0.0)
    kilo_ref[...] = ki.astype(BF16)
    kihi_ref[...] = pltpu.roll(ki, IDX_DIM, 1).astype(BF16)
    wi_scale = (IDX_HEADS ** -0.5) * (IDX_DIM ** -0.5)
    wi_ref[...] = jnp.where(lane < IDX_HEADS, pltpu.roll(sm, LANES - IDX_DIM, 1) * wi_scale, 0.0)


def rope_split(main, small, pos_b, freqs, seq_len):
    T = main.shape[0]
    tm = ROPE_TM
    per_seq = seq_len // tm
    row = lambda n: pl.BlockSpec((tm, n), lambda i: (i, 0))
    vec = pl.BlockSpec((1, LANES), lambda i: (0, 0))
    nq, nk = ATTN_HEADS * HEAD_DIM, ATTN_KV_HEADS * HEAD_DIM
    ni = IDX_HEADS * IDX_DIM
    return pl.pallas_call(
        _rope_kernel,
        grid=(T // tm,),
        in_specs=[row(main.shape[1]), row(LANES), row(LANES), vec],
        out_specs=[row(nq), row(nk), pl.BlockSpec((1, nk, tm), lambda i: (i // per_seq, 0, i % per_seq)),
                   row(ni), row(LANES), row(LANES), row(LANES)],
        out_shape=[jax.ShapeDtypeStruct((T, nq), BF16), jax.ShapeDtypeStruct((T, nk), BF16),
                   jax.ShapeDtypeStruct((T // seq_len, nk, seq_len), BF16), jax.ShapeDtypeStruct((T, ni), BF16),
                   jax.ShapeDtypeStruct((T, LANES), BF16), jax.ShapeDtypeStruct((T, LANES), BF16),
                   jax.ShapeDtypeStruct((T, LANES), F32)],
        compiler_params=_params(("parallel",)),
        name="rope_split",
    )(main, small, pos_b, freqs)


DSA_TQ = 256
DSA_TK = 256
INT_MIN = -2 ** 31


def _dsa_kernel(q_ref, k_ref, vt_ref, qi_ref, kilo_ref, kihi_ref, wi_ref, o_ref,
                scores, m_s, l_s, acc_s):
    tq, tk = DSA_TQ, DSA_TK
    S = k_ref.shape[1]
    topk = min(TOPK_MAX, S // TOPK_KEY_FRACTION)
    q0 = pl.program_id(1) * tq
    n_live = (q0 + tq + tk - 1) // tk
    exp2_scale = (HEAD_DIM ** -0.5) * float(np.log2(np.e))

    key_i = lax.broadcasted_iota(I32, (tk, tq), 0)
    qry_i = q0 + lax.broadcasted_iota(I32, (tk, tq), 1)

    def tile_start(kt):
        return pl.multiple_of(kt * tk, tk)

    def key_to_float(key):
        return pltpu.bitcast(jnp.where(key < 0, key ^ 0x7FFFFFFF, key), F32)

    wit = wi_ref[0].T
    qi_pairs = jnp.concatenate([qi_ref[0, :, c * LANES:(c + 1) * LANES]
                                for c in range(IDX_HEADS * IDX_DIM // LANES)], axis=0)

    def score_tile(kt, carry):
        s0 = tile_start(kt)
        even = _dot_nt(kilo_ref[0, pl.ds(s0, tk), :], qi_pairs)
        odd = _dot_nt(kihi_ref[0, pl.ds(s0, tk), :], qi_pairs)
        score = None
        for h in range(IDX_HEADS):
            logit = (even if h % 2 == 0 else odd)[:, (h // 2) * tq:(h // 2 + 1) * tq]
            term = wit[h:h + 1, :] * jnp.maximum(logit, 0.0)
            score = term if score is None else score + term
        scores[pl.ds(s0, tk), :] = jnp.where(s0 + key_i <= qry_i, score, NEG_INF)
        return carry

    lax.fori_loop(0, n_live, score_tile, 0)

    @pl.when(n_live % 2 == 1)
    def _():
        scores[pl.ds(tile_start(n_live), tk), :] = jnp.full((tk, tq), NEG_INF, F32)

    kf = float(topk)
    ts = 2 * tk
    n_sums = 4

    def count(above):
        def tile(kt, acc):
            hit = jnp.where(above(scores[pl.ds(pl.multiple_of(kt * ts, ts), ts), :]), 1.0, 0.0)
            return acc + jnp.sum(hit.reshape(n_sums, ts // (8 * n_sums), 8, tq), axis=1)
        acc = lax.fori_loop(0, (n_live + 1) // 2, tile, jnp.zeros((n_sums, 8, tq), F32))
        return jnp.sum(acc.reshape(n_sums * 8, tq), axis=0, keepdims=True)

    zero = jnp.zeros((1, tq), I32)
    prefix = jnp.where(count(lambda s: s >= 0.0) >= kf, zero, jnp.full((1, tq), INT_MIN, I32))

    def bit_step(i, prefix):
        cand = prefix | jnp.left_shift(jnp.int32(1), 30 - i)
        cand_f = key_to_float(cand)
        return jnp.where(count(lambda s: s >= cand_f) >= kf, cand, prefix)

    thr_key = lax.fori_loop(0, 31, bit_step, prefix)
    neg_inf_key = INT_MIN + 0x7FFFFF
    thr = jnp.where(thr_key < neg_inf_key, NEG_INF, key_to_float(thr_key))
    need = kf - count(lambda s: s > thr)

    m_s[...] = jnp.full(m_s.shape, -1e30, F32)
    l_s[...] = jnp.zeros(l_s.shape, F32)
    acc_s[...] = jnp.zeros(acc_s.shape, F32)
    before = (lax.broadcasted_iota(I32, (tk, tk), 1) < lax.broadcasted_iota(I32, (tk, tk), 0)).astype(BF16)
    groups = range(ATTN_KV_HEADS)
    q4 = [jnp.concatenate([(q_ref[0, :, (g * GQA_GROUP + u) * HEAD_DIM:(g * GQA_GROUP + u + 1) * HEAD_DIM]
                            .astype(F32) * exp2_scale).astype(BF16)
                           for u in range(GQA_GROUP)], axis=0) for g in groups]
    ones_rows = jnp.ones((16, tk), BF16)

    def attn_tile(kt, ties_seen):
        s0 = tile_start(kt)
        score = scores[pl.ds(s0, tk), :]
        eq = score == thr
        rank = _dot(before, jnp.where(eq, 1.0, 0.0).astype(BF16)) + ties_seen
        sel = ((score > thr) | (eq & (rank < need))) & (s0 + key_i <= qry_i)
        bias = jnp.where(sel, 0.0, -1e30).astype(BF16)
        sts = [_dot_nt(k_ref[0, pl.ds(s0, tk), g * HEAD_DIM:(g + 1) * HEAD_DIM], q4[g]) for g in groups]
        ps, alphas = [], []
        for g in groups:
            p_g, a_g = [], []
            for u in range(GQA_GROUP):
                h = g * GQA_GROUP + u
                s = sts[g][:, u * tq:(u + 1) * tq].astype(BF16) + bias
                m_old = m_s[h]
                m_new = jnp.maximum(m_old, jnp.max(s, axis=0, keepdims=True).astype(F32))
                p_g.append(jnp.exp2(s - m_new.astype(BF16)))
                a_g.append(jnp.exp2(m_old - m_new))
                m_s[h] = m_new
            ps.append(jnp.concatenate(p_g, axis=1))
            alphas.append(jnp.concatenate(a_g, axis=1))
        pvs = [_dot(vt_ref[0, g * HEAD_DIM:(g + 1) * HEAD_DIM, pl.ds(s0, tk)], ps[g]) for g in groups]
        sums = [_dot(ones_rows, ps[g])[0:1, :] for g in groups]
        for g in groups:
            acc_s[g] = alphas[g] * acc_s[g] + pvs[g]
            l_s[g] = alphas[g] * l_s[g] + sums[g]
        return ties_seen + jnp.sum(jnp.where(eq, 1.0, 0.0), axis=0, keepdims=True)

    lax.fori_loop(0, n_live, attn_tile, jnp.zeros((1, tq), F32))

    for g in groups:
        for u in range(GQA_GROUP):
            h = g * GQA_GROUP + u
            cols = slice(u * tq, (u + 1) * tq)
            o_t = acc_s[g, :, cols] / l_s[g, :, cols]
            o_ref[0, :, h * HEAD_DIM:(h + 1) * HEAD_DIM] = o_t.T.astype(o_ref.dtype)


def dsa_attention(q, k, vt, qi, ki_lo, ki_hi, wi):
    B, S, _ = q.shape
    tq = DSA_TQ
    qblk = lambda n: pl.BlockSpec((1, tq, n), lambda b, i: (b, i, 0))
    seq = lambda n: pl.BlockSpec((1, S, n), lambda b, i: (b, 0, 0))
    return pl.pallas_call(
        _dsa_kernel,
        grid=(B, S // tq),
        in_specs=[qblk(q.shape[2]), seq(k.shape[2]),
                  pl.BlockSpec((1, vt.shape[1], S), lambda b, i: (b, 0, 0)),
                  qblk(qi.shape[2]), seq(LANES), seq(LANES), qblk(LANES)],
        out_specs=qblk(q.shape[2]),
        out_shape=jax.ShapeDtypeStruct(q.shape, BF16),
        scratch_shapes=[pltpu.VMEM((S, tq), F32),
                        pltpu.VMEM((ATTN_HEADS, 1, tq), F32),
                        pltpu.VMEM((ATTN_KV_HEADS, 1, GQA_GROUP * tq), F32),
                        pltpu.VMEM((ATTN_KV_HEADS, HEAD_DIM, GQA_GROUP * tq), F32)],
        compiler_params=_params(("parallel", "arbitrary")),
        name="dsa_attention",
    )(q, k, vt, qi, ki_lo, ki_hi, wi)


MM_TM = 1024
MM_TN = 512
OUT_PROJ_TM = 512


def _pad_cols(w, n):
    return jnp.pad(w, ((0, 0), (0, n - w.shape[1])))


def _row(v):
    return v.reshape(1, -1).astype(F32)


def _lane_vec(v):
    return jnp.pad(v.astype(F32), (0, LANES - v.shape[0])).reshape(1, LANES)


def _rope_freqs():
    def both_halves(width):
        half = width // ROPE_FRACTION // 2
        inv_freq = ROPE_THETA ** (-jnp.arange(half, dtype=F32) / half)
        return jnp.concatenate([inv_freq, inv_freq])
    f = jnp.concatenate([both_halves(HEAD_DIM), both_halves(IDX_DIM)])
    return jnp.pad(f, (0, LANES - f.shape[0])).reshape(1, LANES)


def _ffn(x, g, w_gate, w_up, w_down):
    u = ffn_gate_up(x, _row(g), w_gate, w_up, MM_TM, MM_TN)
    return matmul_residual(u, w_down.astype(BF16), x, MM_TM, MM_TN)


def _mixer_ab(xf, B, S, norm_g, w_in, conv_w, a_log, dt_bias, gdn_norm, w_out):
    T = B * S
    gw = GDN_HEADS * HEAD_DIM
    sw = SB_HEADS * HEAD_DIM
    n_small0 = 4 * gw
    w_main = jnp.concatenate([w_in[:, :n_small0], w_in[:, n_small0 + 2 * GDN_HEADS:]], axis=1).astype(BF16)
    w_small = _pad_cols(w_in[:, n_small0:n_small0 + 2 * GDN_HEADS], LANES).astype(BF16)
    g0 = _row(norm_g)
    proj, ab = rms_matmul(xf, g0, w_main, w_main.shape[1], w_small, MM_TM, MM_TN)
    proj = proj.reshape(B, S, -1)
    ab = ab.reshape(B, S, LANES)
    o_a = gdn_mixer(proj, ab, conv_w, _lane_vec(a_log), _lane_vec(dt_bias), _row(gdn_norm))
    o_b = sb_attention(proj, (4 * gw) // HEAD_DIM)
    o = jnp.concatenate([o_a, o_b], axis=-1).reshape(T, gw + sw)
    return matmul_residual(o, w_out.astype(BF16), xf, OUT_PROJ_TM, w_out.shape[1])


def _mixer_c(xf, B, S, positions, norm_g, w_in, w_out):
    T = B * S
    n_main = (ATTN_HEADS + 2 * ATTN_KV_HEADS) * HEAD_DIM + IDX_HEADS * IDX_DIM
    g1 = _row(norm_g)
    main, small = rms_matmul(xf, g1, w_in, n_main, _pad_cols(w_in[:, n_main:], LANES).astype(BF16), MM_TM, MM_TN)
    pos_b = jnp.broadcast_to(positions.reshape(T, 1).astype(F32), (T, LANES))
    q, k, vt, qi, ki_lo, ki_hi, wi = rope_split(main, small, pos_b, _rope_freqs(), S)
    r3 = lambda t: t.reshape(B, S, -1)
    o = dsa_attention(r3(q), r3(k), vt, r3(qi), r3(ki_lo), r3(ki_hi), r3(wi)).reshape(T, -1)
    return matmul_residual(o, w_out.astype(BF16), xf, OUT_PROJ_TM, w_out.shape[1])


def kernel(x, positions, norm_mix, norm_ffn, final_norm, w_in_ab, conv_w_a, a_log, dt_bias, gdn_norm,
           w_out_ab, w_in_c, w_out_c, ffn_gate, ffn_up, ffn_down):
    B, S, D = x.shape
    xf = x.reshape(B * S, D)
    xf = _mixer_ab(xf, B, S, norm_mix[0], w_in_ab[0], conv_w_a[0], a_log[0], dt_bias[0], gdn_norm[0], w_out_ab[0])
    xf = _ffn(xf, norm_ffn[0], ffn_gate[0], ffn_up[0], ffn_down[0])
    xf = _mixer_c(xf, B, S, positions, norm_mix[1], w_in_c[0], w_out_c[0])
    xf = _ffn(xf, norm_ffn[1], ffn_gate[1], ffn_up[1], ffn_down[1])
    return rmsnorm_rows(xf, _row(final_norm), MM_TM).reshape(B, S, D)
```

```python
import functools

import numpy as np
import jax
import jax.numpy as jnp
from jax import lax
from jax.experimental import pallas as pl
from jax.experimental.pallas import tpu as pltpu

F32 = jnp.float32
BF16 = jnp.bfloat16
I32 = jnp.int32

HEAD_DIM = 128
GDN_HEADS = 8
SB_HEADS = 8
GDN_CONV = 4
ATTN_HEADS = 16
ATTN_KV_HEADS = 4
GQA_GROUP = ATTN_HEADS // ATTN_KV_HEADS
IDX_HEADS = 16
IDX_DIM = 64
TOPK_MAX = 256
TOPK_KEY_FRACTION = 4
ROPE_THETA = 500000.0
ROPE_FRACTION = 4
NORM_EPS = 1e-6

LANES = 128
VMEM_LIMIT = 56 * 1024 * 1024
NEG_INF = float("-inf")


def _params(sem):
    return pltpu.CompilerParams(dimension_semantics=sem, vmem_limit_bytes=VMEM_LIMIT)


def _dot(a, b):
    return jnp.dot(a, b, preferred_element_type=F32)


def _dot_nt(a, b):
    return lax.dot_general(a, b, (((1,), (1,)), ((), ())), preferred_element_type=F32)


def _split(a):
    hi = a.astype(BF16)
    lo = (a - hi.astype(F32)).astype(BF16)
    return hi, lo


def _dot_split(a, b):
    ah, al = _split(a)
    bh, bl = _split(b)
    return _dot(jnp.concatenate([ah, al], axis=1), jnp.concatenate([bh, bh], axis=0)) + _dot(ah, bl)


def _softplus(x):
    return jnp.maximum(x, 0.0) + jnp.log1p(jnp.exp(-jnp.abs(x)))


def _sigmoid(x):
    return 1.0 / (1.0 + jnp.exp(-x))


def _rms_matmul_kernel(x_ref, g_ref, w_ref, ws_ref, o_ref, os_ref, h_ref):
    @pl.when(pl.program_id(1) == 0)
    def _():
        x = x_ref[...]
        var = jnp.mean(x * x, axis=-1, keepdims=True)
        h_ref[...] = (x * lax.rsqrt(var + NORM_EPS) * g_ref[...]).astype(BF16)
        os_ref[...] = _dot(h_ref[...], ws_ref[...])

    o_ref[...] = _dot(h_ref[...], w_ref[...].astype(BF16)).astype(o_ref.dtype)


def rms_matmul(x, g, w, n_cols, w_small, tm, tn):
    T, D = x.shape
    tm = min(tm, T)
    N = n_cols
    return pl.pallas_call(
        _rms_matmul_kernel,
        grid=(T // tm, N // tn),
        in_specs=[pl.BlockSpec((tm, D), lambda i, j: (i, 0)),
                  pl.BlockSpec((1, D), lambda i, j: (0, 0)),
                  pl.BlockSpec((D, tn), lambda i, j: (0, j)),
                  pl.BlockSpec((D, LANES), lambda i, j: (0, 0))],
        out_specs=[pl.BlockSpec((tm, tn), lambda i, j: (i, j)),
                   pl.BlockSpec((tm, LANES), lambda i, j: (i, 0))],
        out_shape=[jax.ShapeDtypeStruct((T, N), BF16), jax.ShapeDtypeStruct((T, LANES), F32)],
        scratch_shapes=[pltpu.VMEM((tm, D), BF16)],
        compiler_params=_params(("parallel", "arbitrary")),
        name="rms_matmul",
    )(x, g, w, w_small)


def _ffn_up_kernel(x_ref, g_ref, wg_ref, wu_ref, o_ref, h_ref):
    @pl.when(pl.program_id(1) == 0)
    def _():
        x = x_ref[...]
        var = jnp.mean(x * x, axis=-1, keepdims=True)
        h_ref[...] = (x * lax.rsqrt(var + NORM_EPS) * g_ref[...]).astype(BF16)

    h = h_ref[...]
    a = _dot(h, wg_ref[0].astype(BF16))
    b = _dot(h, wu_ref[0].astype(BF16))
    o_ref[...] = (a * _sigmoid(a) * b).astype(o_ref.dtype)


def ffn_gate_up(x, g, wg, wu, layer, tm, tn):
    T, D = x.shape
    tm = min(tm, T)
    N = wg.shape[2]
    return pl.pallas_call(
        _ffn_up_kernel,
        grid=(T // tm, N // tn),
        in_specs=[pl.BlockSpec((tm, D), lambda i, j: (i, 0)),
                  pl.BlockSpec((1, D), lambda i, j: (0, 0)),
                  pl.BlockSpec((1, D, tn), lambda i, j: (layer, 0, j)),
                  pl.BlockSpec((1, D, tn), lambda i, j: (layer, 0, j))],
        out_specs=pl.BlockSpec((tm, tn), lambda i, j: (i, j)),
        out_shape=jax.ShapeDtypeStruct((T, N), BF16),
        scratch_shapes=[pltpu.VMEM((tm, D), BF16)],
        compiler_params=_params(("parallel", "arbitrary")),
        name="ffn_up",
    )(x, g, wg, wu)


def _matmul_res_kernel(a_ref, w_ref, r_ref, o_ref):
    o_ref[...] = r_ref[...] + _dot(a_ref[...], w_ref[...])


def matmul_residual(a, w, res, tm, tn):
    T, K = a.shape
    tm = min(tm, T)
    N = w.shape[1]
    return pl.pallas_call(
        _matmul_res_kernel,
        grid=(T // tm, N // tn),
        in_specs=[pl.BlockSpec((tm, K), lambda i, j: (i, 0)),
                  pl.BlockSpec((K, tn), lambda i, j: (0, j)),
                  pl.BlockSpec((tm, tn), lambda i, j: (i, j))],
        out_specs=pl.BlockSpec((tm, tn), lambda i, j: (i, j)),
        out_shape=jax.ShapeDtypeStruct((T, N), F32),
        compiler_params=_params(("parallel", "arbitrary")),
        name="matmul_residual",
    )(a, w, res)


def _matmul2_res_kernel(a1_ref, a2_ref, w_ref, r_ref, o_ref):
    k1 = a1_ref.shape[1]
    o_ref[...] = r_ref[...] + _dot(a1_ref[...], w_ref[:k1, :]) + _dot(a2_ref[...], w_ref[k1:, :])


def matmul2_residual(a1, a2, w, res, tm):
    T, K1 = a1.shape
    K2 = a2.shape[1]
    tm = min(tm, T)
    N = w.shape[1]
    return pl.pallas_call(
        _matmul2_res_kernel,
        grid=(T // tm,),
        in_specs=[pl.BlockSpec((tm, K1), lambda i: (i, 0)),
                  pl.BlockSpec((tm, K2), lambda i: (i, 0)),
                  pl.BlockSpec((K1 + K2, N), lambda i: (0, 0)),
                  pl.BlockSpec((tm, N), lambda i: (i, 0))],
        out_specs=pl.BlockSpec((tm, N), lambda i: (i, 0)),
        out_shape=jax.ShapeDtypeStruct((T, N), F32),
        compiler_params=_params(("parallel",)),
        name="matmul2_residual",
    )(a1, a2, w, res)


def _rmsnorm_kernel(x_ref, g_ref, o_ref):
    x = x_ref[...]
    var = jnp.mean(x * x, axis=-1, keepdims=True)
    o_ref[...] = x * lax.rsqrt(var + NORM_EPS) * g_ref[...]


def rmsnorm_rows(x, g, tm):
    T, D = x.shape
    tm = min(tm, T)
    return pl.pallas_call(
        _rmsnorm_kernel,
        grid=(T // tm,),
        in_specs=[pl.BlockSpec((tm, D), lambda i: (i, 0)),
                  pl.BlockSpec((1, D), lambda i: (0, 0))],
        out_specs=pl.BlockSpec((tm, D), lambda i: (i, 0)),
        out_shape=jax.ShapeDtypeStruct((T, D), F32),
        compiler_params=_params(("parallel",)),
        name="final_rmsnorm",
    )(x, g)


GDN_CHUNK = 128
GDN_HEADS_PER_STEP = 2
GDN_PREP_GROUP = 4


def _gdn_kernel(q_ref, k_ref, v_ref, gate_ref, ab_ref, cwq_ref, cwk_ref, cwv_ref,
                alog_ref, dtb_ref, gn_ref, o_ref,
                qn, kn, vn, gb, bb, us, ws, qds, kdts, ints, egl):
    C = GDN_CHUNK
    hb = GDN_HEADS_PER_STEP
    cg = GDN_PREP_GROUP
    S = q_ref.shape[1]
    n_chunks = S // C
    n_groups = n_chunks // cg
    head0 = pl.program_id(1) * hb
    scale = HEAD_DIM ** -0.5
    heads = [slice(hh * HEAD_DIM, (hh + 1) * HEAD_DIM) for hh in range(hb)]

    row_s = lax.broadcasted_iota(I32, (S, LANES), 0)
    lane_s = lax.broadcasted_iota(I32, (S, LANES), 1)

    def conv_silu(x, cw):
        x = x.astype(F32)
        y = x * cw[GDN_CONV - 1:GDN_CONV, :]
        for s in range(1, GDN_CONV):
            xs = jnp.where(row_s >= s, pltpu.roll(x, s, 0), 0.0)
            y = y + xs * cw[GDN_CONV - 1 - s:GDN_CONV - s, :]
        return y * _sigmoid(y)

    def l2n(x):
        return x * lax.rsqrt(jnp.sum(x * x, axis=-1, keepdims=True) + NORM_EPS)

    ab = ab_ref[0]
    g_all = -jnp.exp(alog_ref[...]) * _softplus(ab + dtb_ref[...])
    b_all = _sigmoid(ab)
    for hh, hs in enumerate(heads):
        qn[hh] = l2n(conv_silu(q_ref[0, :, hs], cwq_ref[:, hs]))
        kn[hh] = l2n(conv_silu(k_ref[0, :, hs], cwk_ref[:, hs]))
        vn[hh] = conv_silu(v_ref[0, :, hs], cwv_ref[:, hs])
        g_col = jnp.sum(jnp.where(lane_s == head0 + hh, g_all, 0.0), axis=1, keepdims=True)
        b_col = jnp.sum(jnp.where(lane_s == head0 + hh + GDN_HEADS, b_all, 0.0), axis=1, keepdims=True)
        gb[hh] = jnp.broadcast_to(g_col, (S, LANES))
        bb[hh] = jnp.broadcast_to(b_col, (S, LANES))

    row = lax.broadcasted_iota(I32, (C, C), 0)
    col = lax.broadcasted_iota(I32, (C, C), 1)
    eye = (row == col).astype(F32)
    n_doublings = int(np.log2(C)) - 1

    def prep(i, carry):
        hh = i // n_groups
        c0 = (i % n_groups) * cg
        chunks = range(cg)
        sls = [pl.ds(pl.multiple_of((c0 + j) * C, C), C) for j in chunks]
        gcs = []
        for sl in sls:
            gc = gb[hh, sl, :]
            s = 1
            while s < C:
                gc = gc + jnp.where(row >= s, pltpu.roll(gc, s, 0), 0.0)
                s *= 2
            gcs.append(gc)
        decays = [jnp.exp(jnp.where(row >= col, gc - gc.T, NEG_INF)) for gc in gcs]
        kcs = [kn[hh, sl, :] for sl in sls]
        qcs = [qn[hh, sl, :] * scale for sl in sls]
        bcs = [bb[hh, sl, :] for sl in sls]
        kbs = [kcs[j] * bcs[j] for j in chunks]
        kc16 = [kc.astype(BF16) for kc in kcs]
        kk = [_dot_nt(kbs[j].astype(BF16), kc16[j]) for j in chunks]
        qk = [_dot_nt(qcs[j].astype(BF16), kc16[j]) for j in chunks]
        lowers = [jnp.where(row > col, kk[j] * decays[j], 0.0) for j in chunks]
        for j in chunks:
            ints[hh, sls[j], :] = jnp.where(row >= col, qk[j] * decays[j], 0.0).astype(BF16)
        ps = [_dot_split(l, l) for l in lowers]
        ts = [eye - l for l in lowers]
        for d in range(n_doublings):
            if d < n_doublings - 1:
                ys = [_dot_split(jnp.concatenate([ts[j], ps[j]], axis=0), ps[j]) for j in chunks]
                ts = [ts[j] + ys[j][:C] for j in chunks]
                ps = [ys[j][C:] for j in chunks]
            else:
                ys = [_dot_split(ts[j], ps[j]) for j in chunks]
                ts = [ts[j] + ys[j] for j in chunks]
        egcs = [jnp.exp(gc) for gc in gcs]
        rhs = [jnp.concatenate([vn[hh, sls[j], :] * bcs[j], kbs[j] * egcs[j]], axis=1) for j in chunks]
        sol = [_dot_split(ts[j], rhs[j]) for j in chunks]
        for j in chunks:
            sl = sls[j]
            us[hh, sl, :] = sol[j][:, :HEAD_DIM]
            ws[hh, sl, :] = sol[j][:, HEAD_DIM:].astype(BF16)
            qds[hh, sl, :] = (qcs[j] * egcs[j]).astype(BF16)
            g_last = gcs[j][C - 1:C, :]
            kdts[hh, sl, :] = (kcs[j] * jnp.exp(g_last - gcs[j])).T.astype(BF16)
            egl[hh, pl.ds(pl.multiple_of((c0 + j) * 8, 8), 8), :] = jnp.broadcast_to(jnp.exp(g_last), (8, LANES))
        return carry

    lax.fori_loop(0, hb * n_groups, prep, 0)

    gn = gn_ref[...]
    hrange = range(hb)

    def step(c, states):
        sl = pl.ds(pl.multiple_of(c * C, C), C)
        s16 = [st.astype(BF16) for st in states]
        w_s = [_dot(ws[hh, sl, :], s16[hh]) for hh in hrange]
        q_s = [_dot(qds[hh, sl, :], s16[hh]) for hh in hrange]
        vn16 = [(us[hh, sl, :] - w_s[hh]).astype(BF16) for hh in hrange]
        o_in = [_dot(ints[hh, sl, :], vn16[hh]) for hh in hrange]
        kv = [_dot(kdts[hh, sl, :], vn16[hh]) for hh in hrange]
        new_states = []
        for hh in hrange:
            eg = egl[hh, pl.ds(pl.multiple_of(c * 8, 8), 8), :][0:1, :]
            new_states.append(states[hh] * eg + kv[hh])
            o = q_s[hh] + o_in[hh]
            var = jnp.mean(o * o, axis=-1, keepdims=True)
            gate = gate_ref[0, sl, heads[hh]].astype(F32)
            o_ref[0, sl, heads[hh]] = (o * lax.rsqrt(var + NORM_EPS) * gn * (gate * _sigmoid(gate))).astype(o_ref.dtype)
        return tuple(new_states)

    lax.fori_loop(0, n_chunks, step, tuple(jnp.zeros((HEAD_DIM, HEAD_DIM), F32) for _ in hrange))


def gdn_mixer(proj, ab, conv_w, a_log, dt_bias, gdn_norm):
    B, S, _ = proj.shape
    hb = GDN_HEADS_PER_STEP
    n_grp = GDN_HEADS // hb
    w = hb * HEAD_DIM
    blk = lambda part: pl.BlockSpec((1, S, w), lambda b, g: (b, 0, part * n_grp + g))
    cw = lambda part: pl.BlockSpec((GDN_CONV, w), lambda b, g: (0, part * n_grp + g))
    vec = pl.BlockSpec((1, LANES), lambda b, g: (0, 0))
    seq_f32 = pltpu.VMEM((hb, S, HEAD_DIM), F32)
    seq_b16 = pltpu.VMEM((hb, S, HEAD_DIM), BF16)
    return pl.pallas_call(
        _gdn_kernel,
        grid=(B, n_grp),
        in_specs=[blk(0), blk(1), blk(2), blk(3),
                  pl.BlockSpec((1, S, LANES), lambda b, g: (b, 0, 0)),
                  cw(0), cw(1), cw(2), vec, vec, vec],
        out_specs=pl.BlockSpec((1, S, w), lambda b, g: (b, 0, g)),
        out_shape=jax.ShapeDtypeStruct((B, S, GDN_HEADS * HEAD_DIM), BF16),
        scratch_shapes=[seq_f32, seq_f32, seq_f32, seq_f32, seq_f32, seq_f32,
                        seq_b16, seq_b16, seq_b16, seq_b16,
                        pltpu.VMEM((hb, 8 * (S // GDN_CHUNK), LANES), F32)],
        compiler_params=_params(("parallel", "arbitrary")),
        name="gdn_mixer",
    )(proj, proj, proj, proj, ab, conv_w, conv_w, conv_w, a_log, dt_bias, gdn_norm)


SB_TILE = 128
SB_HEADS_PER_STEP = 8
SB_DEAD = -104.0


def _sb_kernel(q_ref, k_ref, v_ref, o_ref, carry_s, acc_s):
    t = SB_TILE
    hb = SB_HEADS_PER_STEP
    qi = pl.program_id(1)
    scale = HEAD_DIM ** -0.5
    row2 = lax.broadcasted_iota(I32, (t, 2 * t), 0)
    col2 = lax.broadcasted_iota(I32, (t, 2 * t), 1)
    after_ones = jnp.where(col2 >= t, 1.0, jnp.where(row2 > col2, 1.0, 0.0)).astype(BF16)
    after_ones2 = jnp.concatenate([after_ones, after_ones], axis=0)
    causal = lax.broadcasted_iota(I32, (t, t), 1) < lax.broadcasted_iota(I32, (t, t), 0)

    heads = [slice(hh * HEAD_DIM, (hh + 1) * HEAD_DIM) for hh in range(hb)]

    def blocks(s0, diag):
        zs = [_dot_nt(q_ref[0, :, hs], k_ref[0, pl.ds(s0, t), hs]) * scale for hs in heads]
        rs, es = [], []
        for z in zs:
            ls = -(jnp.maximum(z, 0.0) + jnp.log(1.0 + jnp.exp(-jnp.abs(z))))
            if diag:
                ls = jnp.where(causal, ls, 0.0)
            hi, lo = _split(ls)
            rs.append(_dot(jnp.concatenate([hi, lo], axis=1), after_ones2))
            es.append(z + ls)
        pvs = []
        for hh, hs in enumerate(heads):
            between = rs[hh][:, :t] if diag else rs[hh][:, :t] + carry_s[hh]
            a = jnp.exp(es[hh] + between)
            if diag:
                a = jnp.where(causal, a, 0.0)
            pvs.append(_dot(a.astype(BF16), v_ref[0, pl.ds(s0, t), hs]))
        for hh in range(hb):
            if diag:
                acc_s[hh] = pvs[hh]
                carry_s[hh] = rs[hh][:, t:]
            else:
                acc_s[hh] += pvs[hh]
                carry_s[hh] += rs[hh][:, t:]

    def any_live():
        m = carry_s[0]
        for hh in range(1, hb):
            m = jnp.maximum(m, carry_s[hh])
        return jnp.max(m) >= SB_DEAD

    blocks(pl.multiple_of(qi * t, t), True)

    def cond(st):
        return (st[0] < qi) & st[1]

    def body(st):
        j = qi - 1 - st[0]
        blocks(pl.multiple_of(j * t, t), False)
        return st[0] + 1, any_live()

    lax.while_loop(cond, body, (jnp.int32(0), any_live()))
    for hh in range(hb):
        o_ref[0, :, hh * HEAD_DIM:(hh + 1) * HEAD_DIM] = acc_s[hh].astype(o_ref.dtype)


def sb_attention(proj, col0):
    B, S, _ = proj.shape
    hb = SB_HEADS_PER_STEP
    n_grp = SB_HEADS // hb
    t = SB_TILE
    w = hb * HEAD_DIM
    c0 = col0 // hb
    return pl.pallas_call(
        _sb_kernel,
        grid=(B * n_grp, S // t),
        in_specs=[pl.BlockSpec((1, t, w), lambda g, i: (g // n_grp, i, c0 + g % n_grp)),
                  pl.BlockSpec((1, S, w), lambda g, i: (g // n_grp, 0, c0 + n_grp + g % n_grp)),
                  pl.BlockSpec((1, S, w), lambda g, i: (g // n_grp, 0, c0 + 2 * n_grp + g % n_grp))],
        out_specs=pl.BlockSpec((1, t, w), lambda g, i: (g // n_grp, i, g % n_grp)),
        out_shape=jax.ShapeDtypeStruct((B, S, SB_HEADS * HEAD_DIM), BF16),
        scratch_shapes=[pltpu.VMEM((hb, t, t), F32), pltpu.VMEM((hb, t, HEAD_DIM), F32)],
        compiler_params=_params(("parallel", "arbitrary")),
        name="sb_attention",
    )(proj, proj, proj)


ROPE_TM = 512


def _rope_kernel(main_ref, small_ref, pos_ref, freq_ref,
                 q_ref, k_ref, vt_ref, qi_ref, kilo_ref, kihi_ref, wi_ref):
    tm = main_ref.shape[0]
    lane = lax.broadcasted_iota(I32, (tm, LANES), 1)
    half_a = HEAD_DIM // ROPE_FRACTION // 2
    half_i = IDX_DIM // ROPE_FRACTION // 2

    ang = pos_ref[...] * freq_ref[...]
    cos_p = jnp.cos(ang)
    sin_p = jnp.sin(ang)

    def table(c, s, period, half):
        p = lane % period
        return c, jnp.where(p >= half, s, 0.0), jnp.where(p < half, -s, 0.0)

    in_attn = lane < 2 * half_a
    tab_a = table(jnp.where(in_attn, cos_p, 1.0), jnp.where(in_attn, sin_p, 0.0), HEAD_DIM, half_a)

    first = lane < 2 * half_i
    second = (lane >= IDX_DIM) & (lane < IDX_DIM + 2 * half_i)

    def spread(v, fill):
        return jnp.where(first, pltpu.roll(v, LANES - 2 * half_a, 1),
                         jnp.where(second, pltpu.roll(v, IDX_DIM - 2 * half_a, 1), fill))

    tab_i = table(spread(cos_p, 1.0), spread(sin_p, 0.0), IDX_DIM, half_i)

    def rot(x, tab, half):
        c, s_up, s_dn = tab
        return x * c + pltpu.roll(x, half, 1) * s_up + pltpu.roll(x, LANES - half, 1) * s_dn

    nq = ATTN_HEADS
    nk = ATTN_KV_HEADS
    for c in range(nq):
        x = main_ref[:, c * LANES:(c + 1) * LANES].astype(F32)
        q_ref[:, c * LANES:(c + 1) * LANES] = rot(x, tab_a, half_a).astype(BF16)
    for c in range(nk):
        x = main_ref[:, (nq + c) * LANES:(nq + c + 1) * LANES].astype(F32)
        k_ref[:, c * LANES:(c + 1) * LANES] = rot(x, tab_a, half_a).astype(BF16)
    vt_ref[0] = main_ref[:, (nq + nk) * LANES:(nq + 2 * nk) * LANES].astype(F32).T.astype(BF16)
    c0 = nq + 2 * nk
    for c in range(IDX_HEADS * IDX_DIM // LANES):
        x = main_ref[:, (c0 + c) * LANES:(c0 + c + 1) * LANES].astype(F32)
        qi_ref[:, c * LANES:(c + 1) * LANES] = rot(x, tab_i, half_i).astype(BF16)
    sm = small_ref[...]
    ki = jnp.where(lane < IDX_DIM, rot(sm, tab_i, half_i), 0.0)
    kilo_ref[...] = ki.astype(BF16)
    kihi_ref[...] = pltpu.roll(ki, IDX_DIM, 1).astype(BF16)
    wi_scale = (IDX_HEADS ** -0.5) * (IDX_DIM ** -0.5)
    wi_ref[...] = jnp.where(lane < IDX_HEADS, pltpu.roll(sm, LANES - IDX_DIM, 1) * wi_scale, 0.0)


def rope_split(main, small, pos_b, freqs, seq_len):
    T = main.shape[0]
    tm = ROPE_TM
    per_seq = seq_len // tm
    row = lambda n: pl.BlockSpec((tm, n), lambda i: (i, 0))
    vec = pl.BlockSpec((1, LANES), lambda i: (0, 0))
    nq, nk = ATTN_HEADS * HEAD_DIM, ATTN_KV_HEADS * HEAD_DIM
    ni = IDX_HEADS * IDX_DIM
    return pl.pallas_call(
        _rope_kernel,
        grid=(T // tm,),
        in_specs=[row(main.shape[1]), row(LANES), row(LANES), vec],
        out_specs=[row(nq), row(nk), pl.BlockSpec((1, nk, tm), lambda i: (i // per_seq, 0, i % per_seq)),
                   row(ni), row(LANES), row(LANES), row(LANES)],
        out_shape=[jax.ShapeDtypeStruct((T, nq), BF16), jax.ShapeDtypeStruct((T, nk), BF16),
                   jax.ShapeDtypeStruct((T // seq_len, nk, seq_len), BF16), jax.ShapeDtypeStruct((T, ni), BF16),
                   jax.ShapeDtypeStruct((T, LANES), BF16), jax.ShapeDtypeStruct((T, LANES), BF16),
                   jax.ShapeDtypeStruct((T, LANES), F32)],
        compiler_params=_params(("parallel",)),
        name="rope_split",
    )(main, small, pos_b, freqs)


DSA_TQ = 256
DSA_TK = 256
INT_MIN = -2 ** 31


def _dsa_kernel(q_ref, k_ref, vt_ref, qi_ref, kilo_ref, kihi_ref, wi_ref, o_ref,
                scores, m_s, l_s, acc_s):
    tq, tk = DSA_TQ, DSA_TK
    S = k_ref.shape[1]
    topk = min(TOPK_MAX, S // TOPK_KEY_FRACTION)
    q0 = pl.program_id(1) * tq
    n_live = (q0 + tq + tk - 1) // tk
    exp2_scale = (HEAD_DIM ** -0.5) * float(np.log2(np.e))

    key_i = lax.broadcasted_iota(I32, (tk, tq), 0)
    qry_i = q0 + lax.broadcasted_iota(I32, (tk, tq), 1)

    def tile_start(kt):
        return pl.multiple_of(kt * tk, tk)

    def key_to_float(key):
        return pltpu.bitcast(jnp.where(key < 0, key ^ 0x7FFFFFFF, key), F32)

    wit = wi_ref[0].T
    qi_pairs = jnp.concatenate([qi_ref[0, :, c * LANES:(c + 1) * LANES]
                                for c in range(IDX_HEADS * IDX_DIM // LANES)], axis=0)

    def score_tile(kt, carry):
        s0 = tile_start(kt)
        even = _dot_nt(kilo_ref[0, pl.ds(s0, tk), :], qi_pairs)
        odd = _dot_nt(kihi_ref[0, pl.ds(s0, tk), :], qi_pairs)
        score = None
        for h in range(IDX_HEADS):
            logit = (even if h % 2 == 0 else odd)[:, (h // 2) * tq:(h // 2 + 1) * tq]
            term = wit[h:h + 1, :] * jnp.maximum(logit, 0.0)
            score = term if score is None else score + term
        scores[pl.ds(s0, tk), :] = jnp.where(s0 + key_i <= qry_i, score, NEG_INF)
        return carry

    lax.fori_loop(0, n_live, score_tile, 0)

    @pl.when(n_live % 2 == 1)
    def _():
        scores[pl.ds(tile_start(n_live), tk), :] = jnp.full((tk, tq), NEG_INF, F32)

    kf = float(topk)
    ts = 2 * tk
    n_sums = 4

    def count(above):
        def tile(kt, acc):
            hit = jnp.where(above(scores[pl.ds(pl.multiple_of(kt * ts, ts), ts), :]), 1.0, 0.0)
            return acc + jnp.sum(hit.reshape(n_sums, ts // (8 * n_sums), 8, tq), axis=1)
        acc = lax.fori_loop(0, (n_live + 1) // 2, tile, jnp.zeros((n_sums, 8, tq), F32))
        return jnp.sum(acc.reshape(n_sums * 8, tq), axis=0, keepdims=True)

    zero = jnp.zeros((1, tq), I32)
    prefix = jnp.where(count(lambda s: s >= 0.0) >= kf, zero, jnp.full((1, tq), INT_MIN, I32))

    def bit_step(i, prefix):
        cand = prefix | jnp.left_shift(jnp.int32(1), 30 - i)
        cand_f = key_to_float(cand)
        return jnp.where(count(lambda s: s >= cand_f) >= kf, cand, prefix)

    thr_key = lax.fori_loop(0, 31, bit_step, prefix)
    neg_inf_key = INT_MIN + 0x7FFFFF
    thr = jnp.where(thr_key < neg_inf_key, NEG_INF, key_to_float(thr_key))
    need = kf - count(lambda s: s > thr)

    m_s[...] = jnp.full(m_s.shape, -1e30, F32)
    l_s[...] = jnp.zeros(l_s.shape, F32)
    acc_s[...] = jnp.zeros(acc_s.shape, F32)
    before = (lax.broadcasted_iota(I32, (tk, tk), 1) < lax.broadcasted_iota(I32, (tk, tk), 0)).astype(BF16)
    groups = range(ATTN_KV_HEADS)
    q4 = [jnp.concatenate([(q_ref[0, :, (g * GQA_GROUP + u) * HEAD_DIM:(g * GQA_GROUP + u + 1) * HEAD_DIM]
                            .astype(F32) * exp2_scale).astype(BF16)
                           for u in range(GQA_GROUP)], axis=0) for g in groups]
    ones_rows = jnp.ones((16, tk), BF16)

    def attn_tile(kt, ties_seen):
        s0 = tile_start(kt)
        score = scores[pl.ds(s0, tk), :]
        eq = score == thr
        rank = _dot(before, jnp.where(eq, 1.0, 0.0).astype(BF16)) + ties_seen
        sel = ((score > thr) | (eq & (rank < need))) & (s0 + key_i <= qry_i)
        bias = jnp.where(sel, 0.0, -1e30).astype(BF16)
        sts = [_dot_nt(k_ref[0, pl.ds(s0, tk), g * HEAD_DIM:(g + 1) * HEAD_DIM], q4[g]) for g in groups]
        ps, alphas = [], []
        for g in groups:
            p_g, a_g = [], []
            for u in range(GQA_GROUP):
                h = g * GQA_GROUP + u
                s = sts[g][:, u * tq:(u + 1) * tq].astype(BF16) + bias
                m_old = m_s[h]
                m_new = jnp.maximum(m_old, jnp.max(s, axis=0, keepdims=True).astype(F32))
                p_g.append(jnp.exp2(s - m_new.astype(BF16)))
                a_g.append(jnp.exp2(m_old - m_new))
                m_s[h] = m_new
            ps.append(jnp.concatenate(p_g, axis=1))
            alphas.append(jnp.concatenate(a_g, axis=1))
        pvs = [_dot(vt_ref[0, g * HEAD_DIM:(g + 1) * HEAD_DIM, pl.ds(s0, tk)], ps[g]) for g in groups]
        sums = [_dot(ones_rows, ps[g])[0:1, :] for g in groups]
        for g in groups:
            acc_s[g] = alphas[g] * acc_s[g] + pvs[g]
            l_s[g] = alphas[g] * l_s[g] + sums[g]
        return ties_seen + jnp.sum(jnp.where(eq, 1.0, 0.0), axis=0, keepdims=True)

    lax.fori_loop(0, n_live, attn_tile, jnp.zeros((1, tq), F32))

    for g in groups:
        for u in range(GQA_GROUP):
            h = g * GQA_GROUP + u
            cols = slice(u * tq, (u + 1) * tq)
            o_t = acc_s[g, :, cols] / l_s[g, :, cols]
            o_ref[0, :, h * HEAD_DIM:(h + 1) * HEAD_DIM] = o_t.T.astype(o_ref.dtype)


def dsa_attention(q, k, vt, qi, ki_lo, ki_hi, wi):
    B, S, _ = q.shape
    tq = DSA_TQ
    qblk = lambda n: pl.BlockSpec((1, tq, n), lambda b, i: (b, i, 0))
    seq = lambda n: pl.BlockSpec((1, S, n), lambda b, i: (b, 0, 0))
    return pl.pallas_call(
        _dsa_kernel,
        grid=(B, S // tq),
        in_specs=[qblk(q.shape[2]), seq(k.shape[2]),
                  pl.BlockSpec((1, vt.shape[1], S), lambda b, i: (b, 0, 0)),
                  qblk(qi.shape[2]), seq(LANES), seq(LANES), qblk(LANES)],
        out_specs=qblk(q.shape[2]),
        out_shape=jax.ShapeDtypeStruct(q.shape, BF16),
        scratch_shapes=[pltpu.VMEM((S, tq), F32),
                        pltpu.VMEM((ATTN_HEADS, 1, tq), F32),
                        pltpu.VMEM((ATTN_KV_HEADS, 1, GQA_GROUP * tq), F32),
                        pltpu.VMEM((ATTN_KV_HEADS, HEAD_DIM, GQA_GROUP * tq), F32)],
        compiler_params=_params(("parallel", "arbitrary")),
        name="dsa_attention",
    )(q, k, vt, qi, ki_lo, ki_hi, wi)


MM_TM = 1024
MM_TN = 512
OUT_PROJ_TM = 512


def _pad_cols(w, n):
    return jnp.pad(w, ((0, 0), (0, n - w.shape[1])))


def _row(v):
    return v.reshape(1, -1).astype(F32)


def _lane_vec(v):
    return jnp.pad(v.astype(F32), (0, LANES - v.shape[0])).reshape(1, LANES)


def _rope_freqs():
    def both_halves(width):
        half = width // ROPE_FRACTION // 2
        inv_freq = ROPE_THETA ** (-jnp.arange(half, dtype=F32) / half)
        return jnp.concatenate([inv_freq, inv_freq])
    f = jnp.concatenate([both_halves(HEAD_DIM), both_halves(IDX_DIM)])
    return jnp.pad(f, (0, LANES - f.shape[0])).reshape(1, LANES)


def _ffn(x, g, w_gate, w_up, w_down, layer):
    u = ffn_gate_up(x, _row(g), w_gate, w_up, layer, MM_TM, MM_TN)
    return matmul_residual(u, w_down[layer].astype(BF16), x, MM_TM, MM_TN)


def _mixer_ab(xf, B, S, norm_g, w_in, conv_w, a_log, dt_bias, gdn_norm, w_out):
    T = B * S
    gw = GDN_HEADS * HEAD_DIM
    sw = SB_HEADS * HEAD_DIM
    n_small0 = 4 * gw
    w_main = jnp.concatenate([w_in[:, :n_small0], w_in[:, n_small0 + 2 * GDN_HEADS:]], axis=1).astype(BF16)
    w_small = _pad_cols(w_in[:, n_small0:n_small0 + 2 * GDN_HEADS], LANES).astype(BF16)
    g0 = _row(norm_g)
    proj, ab = rms_matmul(xf, g0, w_main, w_main.shape[1], w_small, MM_TM, MM_TN)
    proj = proj.reshape(B, S, -1)
    ab = ab.reshape(B, S, LANES)
    o_a = gdn_mixer(proj, ab, conv_w, _lane_vec(a_log), _lane_vec(dt_bias), _row(gdn_norm))
    o_b = sb_attention(proj, (4 * gw) // HEAD_DIM)
    return matmul2_residual(o_a.reshape(T, gw), o_b.reshape(T, sw), w_out.astype(BF16), xf, OUT_PROJ_TM)


def _mixer_c(xf, B, S, positions, norm_g, w_in, w_out):
    T = B * S
    n_main = (ATTN_HEADS + 2 * ATTN_KV_HEADS) * HEAD_DIM + IDX_HEADS * IDX_DIM
    g1 = _row(norm_g)
    main, small = rms_matmul(xf, g1, w_in, n_main, _pad_cols(w_in[:, n_main:], LANES).astype(BF16), MM_TM, MM_TN)
    pos_b = jnp.broadcast_to(positions.reshape(T, 1).astype(F32), (T, LANES))
    q, k, vt, qi, ki_lo, ki_hi, wi = rope_split(main, small, pos_b, _rope_freqs(), S)
    r3 = lambda t: t.reshape(B, S, -1)
    o = dsa_attention(r3(q), r3(k), vt, r3(qi), r3(ki_lo), r3(ki_hi), r3(wi)).reshape(T, -1)
    return matmul_residual(o, w_out.astype(BF16), xf, OUT_PROJ_TM, w_out.shape[1])


def kernel(x, positions, norm_mix, norm_ffn, final_norm, w_in_ab, conv_w_a, a_log, dt_bias, gdn_norm,
           w_out_ab, w_in_c, w_out_c, ffn_gate, ffn_up, ffn_down):
    B, S, D = x.shape
    xf = x.reshape(B * S, D)
    xf = _mixer_ab(xf, B, S, norm_mix[0], w_in_ab[0], conv_w_a[0], a_log[0], dt_bias[0], gdn_norm[0], w_out_ab[0])
    xf = _ffn(xf, norm_ffn[0], ffn_gate, ffn_up, ffn_down, 0)
    xf = _mixer_c(xf, B, S, positions, norm_mix[1], w_in_c[0], w_out_c[0])
    xf = _ffn(xf, norm_ffn[1], ffn_gate, ffn_up, ffn_down, 1)
    return rmsnorm_rows(xf, _row(final_norm), MM_TM).reshape(B, S, D)
```

```python
import functools

import numpy as np
import jax
import jax.numpy as jnp
from jax import lax
from jax.experimental import pallas as pl
from jax.experimental.pallas import tpu as pltpu

F32 = jnp.float32
BF16 = jnp.bfloat16
I32 = jnp.int32

HEAD_DIM = 128
GDN_HEADS = 8
SB_HEADS = 8
GDN_CONV = 4
ATTN_HEADS = 16
ATTN_KV_HEADS = 4
GQA_GROUP = ATTN_HEADS // ATTN_KV_HEADS
IDX_HEADS = 16
IDX_DIM = 64
TOPK_MAX = 256
TOPK_KEY_FRACTION = 4
ROPE_THETA = 500000.0
ROPE_FRACTION = 4
NORM_EPS = 1e-6

LANES = 128
VMEM_LIMIT = 56 * 1024 * 1024
NEG_INF = float("-inf")


def _params(sem):
    return pltpu.CompilerParams(dimension_semantics=sem, vmem_limit_bytes=VMEM_LIMIT)


def _dot(a, b):
    return jnp.dot(a, b, preferred_element_type=F32)


def _dot_nt(a, b):
    return lax.dot_general(a, b, (((1,), (1,)), ((), ())), preferred_element_type=F32)


def _split(a):
    hi = a.astype(BF16)
    lo = (a - hi.astype(F32)).astype(BF16)
    return hi, lo


def _dot_split(a, b):
    ah, al = _split(a)
    bh, bl = _split(b)
    return _dot(jnp.concatenate([ah, al], axis=1), jnp.concatenate([bh, bh], axis=0)) + _dot(ah, bl)


def _softplus(x):
    return jnp.maximum(x, 0.0) + jnp.log1p(jnp.exp(-jnp.abs(x)))


def _sigmoid(x):
    return 1.0 / (1.0 + jnp.exp(-x))


def _rms_matmul_kernel(x_ref, g_ref, w_ref, ws_ref, o_ref, os_ref, h_ref):
    @pl.when(pl.program_id(1) == 0)
    def _():
        x = x_ref[...]
        var = jnp.mean(x * x, axis=-1, keepdims=True)
        h_ref[...] = (x * lax.rsqrt(var + NORM_EPS) * g_ref[...]).astype(BF16)
        os_ref[...] = _dot(h_ref[...], ws_ref[...])

    o_ref[...] = _dot(h_ref[...], w_ref[...].astype(BF16)).astype(o_ref.dtype)


def rms_matmul(x, g, w, n_cols, w_small, tm, tn):
    T, D = x.shape
    tm = min(tm, T)
    N = n_cols
    return pl.pallas_call(
        _rms_matmul_kernel,
        grid=(T // tm, N // tn),
        in_specs=[pl.BlockSpec((tm, D), lambda i, j: (i, 0)),
                  pl.BlockSpec((1, D), lambda i, j: (0, 0)),
                  pl.BlockSpec((D, tn), lambda i, j: (0, j)),
                  pl.BlockSpec((D, LANES), lambda i, j: (0, 0))],
        out_specs=[pl.BlockSpec((tm, tn), lambda i, j: (i, j)),
                   pl.BlockSpec((tm, LANES), lambda i, j: (i, 0))],
        out_shape=[jax.ShapeDtypeStruct((T, N), BF16), jax.ShapeDtypeStruct((T, LANES), F32)],
        scratch_shapes=[pltpu.VMEM((tm, D), BF16)],
        compiler_params=_params(("parallel", "arbitrary")),
        name="rms_matmul",
    )(x, g, w, w_small)


def _ffn_up_kernel(x_ref, g_ref, wg_ref, wu_ref, o_ref, h_ref):
    @pl.when(pl.program_id(1) == 0)
    def _():
        x = x_ref[...]
        var = jnp.mean(x * x, axis=-1, keepdims=True)
        h_ref[...] = (x * lax.rsqrt(var + NORM_EPS) * g_ref[...]).astype(BF16)

    h = h_ref[...]
    a = _dot(h, wg_ref[0].astype(BF16))
    b = _dot(h, wu_ref[0].astype(BF16))
    o_ref[...] = (a * _sigmoid(a) * b).astype(o_ref.dtype)


def ffn_gate_up(x, g, wg, wu, layer, tm, tn):
    T, D = x.shape
    tm = min(tm, T)
    N = wg.shape[2]
    return pl.pallas_call(
        _ffn_up_kernel,
        grid=(T // tm, N // tn),
        in_specs=[pl.BlockSpec((tm, D), lambda i, j: (i, 0)),
                  pl.BlockSpec((1, D), lambda i, j: (0, 0)),
                  pl.BlockSpec((1, D, tn), lambda i, j: (layer, 0, j)),
                  pl.BlockSpec((1, D, tn), lambda i, j: (layer, 0, j))],
        out_specs=pl.BlockSpec((tm, tn), lambda i, j: (i, j)),
        out_shape=jax.ShapeDtypeStruct((T, N), BF16),
        scratch_shapes=[pltpu.VMEM((tm, D), BF16)],
        compiler_params=_params(("parallel", "arbitrary")),
        name="ffn_up",
    )(x, g, wg, wu)


def _matmul_res_kernel(a_ref, w_ref, r_ref, o_ref):
    o_ref[...] = r_ref[...] + _dot(a_ref[...], w_ref[...])


def matmul_residual(a, w, res, tm, tn):
    T, K = a.shape
    tm = min(tm, T)
    N = w.shape[1]
    return pl.pallas_call(
        _matmul_res_kernel,
        grid=(T // tm, N // tn),
        in_specs=[pl.BlockSpec((tm, K), lambda i, j: (i, 0)),
                  pl.BlockSpec((K, tn), lambda i, j: (0, j)),
                  pl.BlockSpec((tm, tn), lambda i, j: (i, j))],
        out_specs=pl.BlockSpec((tm, tn), lambda i, j: (i, j)),
        out_shape=jax.ShapeDtypeStruct((T, N), F32),
        compiler_params=_params(("parallel", "arbitrary")),
        name="matmul_residual",
    )(a, w, res)


def _matmul2_res_kernel(a1_ref, a2_ref, w_ref, r_ref, o_ref):
    k1 = a1_ref.shape[1]
    o_ref[...] = r_ref[...] + _dot(a1_ref[...], w_ref[:k1, :]) + _dot(a2_ref[...], w_ref[k1:, :])


def matmul2_residual(a1, a2, w, res, tm):
    T, K1 = a1.shape
    K2 = a2.shape[1]
    tm = min(tm, T)
    N = w.shape[1]
    return pl.pallas_call(
        _matmul2_res_kernel,
        grid=(T // tm,),
        in_specs=[pl.BlockSpec((tm, K1), lambda i: (i, 0)),
                  pl.BlockSpec((tm, K2), lambda i: (i, 0)),
                  pl.BlockSpec((K1 + K2, N), lambda i: (0, 0)),
                  pl.BlockSpec((tm, N), lambda i: (i, 0))],
        out_specs=pl.BlockSpec((tm, N), lambda i: (i, 0)),
        out_shape=jax.ShapeDtypeStruct((T, N), F32),
        compiler_params=_params(("parallel",)),
        name="matmul2_residual",
    )(a1, a2, w, res)


def _rmsnorm_kernel(x_ref, g_ref, o_ref):
    x = x_ref[...]
    var = jnp.mean(x * x, axis=-1, keepdims=True)
    o_ref[...] = x * lax.rsqrt(var + NORM_EPS) * g_ref[...]


def rmsnorm_rows(x, g, tm):
    T, D = x.shape
    tm = min(tm, T)
    return pl.pallas_call(
        _rmsnorm_kernel,
        grid=(T // tm,),
        in_specs=[pl.BlockSpec((tm, D), lambda i: (i, 0)),
                  pl.BlockSpec((1, D), lambda i: (0, 0))],
        out_specs=pl.BlockSpec((tm, D), lambda i: (i, 0)),
        out_shape=jax.ShapeDtypeStruct((T, D), F32),
        compiler_params=_params(("parallel",)),
        name="final_rmsnorm",
    )(x, g)


GDN_CHUNK = 128
GDN_HEADS_PER_STEP = 2
GDN_PREP_GROUP = 8


def _gdn_kernel(q_ref, k_ref, v_ref, gate_ref, ab_ref, cwq_ref, cwk_ref, cwv_ref,
                alog_ref, dtb_ref, gn_ref, o_ref,
                qn, kn, vn, gb, bb, us, ws, qds, kdts, ints, egl):
    C = GDN_CHUNK
    hb = GDN_HEADS_PER_STEP
    S = q_ref.shape[1]
    n_chunks = S // C
    cg = min(GDN_PREP_GROUP, n_chunks)
    n_groups = n_chunks // cg
    head0 = pl.program_id(1) * hb
    scale = HEAD_DIM ** -0.5
    heads = [slice(hh * HEAD_DIM, (hh + 1) * HEAD_DIM) for hh in range(hb)]

    row_s = lax.broadcasted_iota(I32, (S, LANES), 0)
    lane_s = lax.broadcasted_iota(I32, (S, LANES), 1)

    def conv_silu(x, cw):
        x = x.astype(F32)
        y = x * cw[GDN_CONV - 1:GDN_CONV, :]
        for s in range(1, GDN_CONV):
            xs = jnp.where(row_s >= s, pltpu.roll(x, s, 0), 0.0)
            y = y + xs * cw[GDN_CONV - 1 - s:GDN_CONV - s, :]
        return y * _sigmoid(y)

    def l2n(x):
        return x * lax.rsqrt(jnp.sum(x * x, axis=-1, keepdims=True) + NORM_EPS)

    ab = ab_ref[0]
    g_all = -jnp.exp(alog_ref[...]) * _softplus(ab + dtb_ref[...])
    b_all = _sigmoid(ab)
    for hh, hs in enumerate(heads):
        qn[hh] = l2n(conv_silu(q_ref[0, :, hs], cwq_ref[:, hs]))
        kn[hh] = l2n(conv_silu(k_ref[0, :, hs], cwk_ref[:, hs]))
        vn[hh] = conv_silu(v_ref[0, :, hs], cwv_ref[:, hs])
        g_col = jnp.sum(jnp.where(lane_s == head0 + hh, g_all, 0.0), axis=1, keepdims=True)
        b_col = jnp.sum(jnp.where(lane_s == head0 + hh + GDN_HEADS, b_all, 0.0), axis=1, keepdims=True)
        gb[hh] = jnp.broadcast_to(g_col, (S, LANES))
        bb[hh] = jnp.broadcast_to(b_col, (S, LANES))

    row = lax.broadcasted_iota(I32, (C, C), 0)
    col = lax.broadcasted_iota(I32, (C, C), 1)
    eye = (row == col).astype(F32)
    n_doublings = int(np.log2(C)) - 1

    def prep(i, carry):
        hh = i // n_groups
        c0 = (i % n_groups) * cg
        chunks = range(cg)
        sls = [pl.ds(pl.multiple_of((c0 + j) * C, C), C) for j in chunks]
        gcs = []
        for sl in sls:
            gc = gb[hh, sl, :]
            s = 1
            while s < C:
                gc = gc + jnp.where(row >= s, pltpu.roll(gc, s, 0), 0.0)
                s *= 2
            gcs.append(gc)
        decays = [jnp.exp(jnp.where(row >= col, gc - gc.T, NEG_INF)) for gc in gcs]
        kcs = [kn[hh, sl, :] for sl in sls]
        qcs = [qn[hh, sl, :] * scale for sl in sls]
        bcs = [bb[hh, sl, :] for sl in sls]
        kbs = [kcs[j] * bcs[j] for j in chunks]
        kc16 = [kc.astype(BF16) for kc in kcs]
        kk = [_dot_nt(kbs[j].astype(BF16), kc16[j]) for j in chunks]
        qk = [_dot_nt(qcs[j].astype(BF16), kc16[j]) for j in chunks]
        lowers = [jnp.where(row > col, kk[j] * decays[j], 0.0) for j in chunks]
        for j in chunks:
            ints[hh, sls[j], :] = jnp.where(row >= col, qk[j] * decays[j], 0.0).astype(BF16)
        ps = [_dot_split(l, l) for l in lowers]
        ts = [eye - l for l in lowers]
        for d in range(n_doublings):
            if d < n_doublings - 1:
                ys = [_dot_split(jnp.concatenate([ts[j], ps[j]], axis=0), ps[j]) for j in chunks]
                ts = [ts[j] + ys[j][:C] for j in chunks]
                ps = [ys[j][C:] for j in chunks]
            else:
                ys = [_dot_split(ts[j], ps[j]) for j in chunks]
                ts = [ts[j] + ys[j] for j in chunks]
        egcs = [jnp.exp(gc) for gc in gcs]
        rhs = [jnp.concatenate([vn[hh, sls[j], :] * bcs[j], kbs[j] * egcs[j]], axis=1) for j in chunks]
        sol = [_dot_split(ts[j], rhs[j]) for j in chunks]
        for j in chunks:
            sl = sls[j]
            us[hh, sl, :] = sol[j][:, :HEAD_DIM]
            ws[hh, sl, :] = sol[j][:, HEAD_DIM:].astype(BF16)
            qds[hh, sl, :] = (qcs[j] * egcs[j]).astype(BF16)
            g_last = gcs[j][C - 1:C, :]
            kdts[hh, sl, :] = (kcs[j] * jnp.exp(g_last - gcs[j])).T.astype(BF16)
            egl[hh, pl.ds(pl.multiple_of((c0 + j) * 8, 8), 8), :] = jnp.broadcast_to(jnp.exp(g_last), (8, LANES))
        return carry

    lax.fori_loop(0, hb * n_groups, prep, 0)

    gn = gn_ref[...]
    hrange = range(hb)

    def step(c, states):
        sl = pl.ds(pl.multiple_of(c * C, C), C)
        s16 = [st.astype(BF16) for st in states]
        w_s = [_dot(ws[hh, sl, :], s16[hh]) for hh in hrange]
        q_s = [_dot(qds[hh, sl, :], s16[hh]) for hh in hrange]
        vn16 = [(us[hh, sl, :] - w_s[hh]).astype(BF16) for hh in hrange]
        o_in = [_dot(ints[hh, sl, :], vn16[hh]) for hh in hrange]
        kv = [_dot(kdts[hh, sl, :], vn16[hh]) for hh in hrange]
        new_states = []
        for hh in hrange:
            eg = egl[hh, pl.ds(pl.multiple_of(c * 8, 8), 8), :][0:1, :]
            new_states.append(states[hh] * eg + kv[hh])
            o = q_s[hh] + o_in[hh]
            var = jnp.mean(o * o, axis=-1, keepdims=True)
            gate = gate_ref[0, sl, heads[hh]].astype(F32)
            o_ref[0, sl, heads[hh]] = (o * lax.rsqrt(var + NORM_EPS) * gn * (gate * _sigmoid(gate))).astype(o_ref.dtype)
        return tuple(new_states)

    lax.fori_loop(0, n_chunks, step, tuple(jnp.zeros((HEAD_DIM, HEAD_DIM), F32) for _ in hrange))


def gdn_mixer(proj, ab, conv_w, a_log, dt_bias, gdn_norm):
    B, S, _ = proj.shape
    hb = GDN_HEADS_PER_STEP
    n_grp = GDN_HEADS // hb
    w = hb * HEAD_DIM
    blk = lambda part: pl.BlockSpec((1, S, w), lambda b, g: (b, 0, part * n_grp + g))
    cw = lambda part: pl.BlockSpec((GDN_CONV, w), lambda b, g: (0, part * n_grp + g))
    vec = pl.BlockSpec((1, LANES), lambda b, g: (0, 0))
    seq_f32 = pltpu.VMEM((hb, S, HEAD_DIM), F32)
    seq_b16 = pltpu.VMEM((hb, S, HEAD_DIM), BF16)
    return pl.pallas_call(
        _gdn_kernel,
        grid=(B, n_grp),
        in_specs=[blk(0), blk(1), blk(2), blk(3),
                  pl.BlockSpec((1, S, LANES), lambda b, g: (b, 0, 0)),
                  cw(0), cw(1), cw(2), vec, vec, vec],
        out_specs=pl.BlockSpec((1, S, w), lambda b, g: (b, 0, g)),
        out_shape=jax.ShapeDtypeStruct((B, S, GDN_HEADS * HEAD_DIM), BF16),
        scratch_shapes=[seq_f32, seq_f32, seq_f32, seq_f32, seq_f32, seq_f32,
                        seq_b16, seq_b16, seq_b16, seq_b16,
                        pltpu.VMEM((hb, 8 * (S // GDN_CHUNK), LANES), F32)],
        compiler_params=_params(("parallel", "arbitrary")),
        name="gdn_mixer",
    )(proj, proj, proj, proj, ab, conv_w, conv_w, conv_w, a_log, dt_bias, gdn_norm)


SB_TILE = 128
SB_HEADS_PER_STEP = 8
SB_DEAD = -104.0


def _sb_kernel(q_ref, k_ref, v_ref, o_ref, carry_s, acc_s):
    t = SB_TILE
    hb = SB_HEADS_PER_STEP
    qi = pl.program_id(1)
    scale = HEAD_DIM ** -0.5
    row2 = lax.broadcasted_iota(I32, (t, 2 * t), 0)
    col2 = lax.broadcasted_iota(I32, (t, 2 * t), 1)
    after_ones = jnp.where(col2 >= t, 1.0, jnp.where(row2 > col2, 1.0, 0.0)).astype(BF16)
    after_ones2 = jnp.concatenate([after_ones, after_ones], axis=0)
    causal = lax.broadcasted_iota(I32, (t, t), 1) < lax.broadcasted_iota(I32, (t, t), 0)

    heads = [slice(hh * HEAD_DIM, (hh + 1) * HEAD_DIM) for hh in range(hb)]

    def blocks(s0, diag):
        zs = [_dot_nt(q_ref[0, :, hs], k_ref[0, pl.ds(s0, t), hs]) * scale for hs in heads]
        rs, es = [], []
        for z in zs:
            ls = -(jnp.maximum(z, 0.0) + jnp.log(1.0 + jnp.exp(-jnp.abs(z))))
            if diag:
                ls = jnp.where(causal, ls, 0.0)
            hi, lo = _split(ls)
            rs.append(_dot(jnp.concatenate([hi, lo], axis=1), after_ones2))
            es.append(z + ls)
        pvs = []
        for hh, hs in enumerate(heads):
            between = rs[hh][:, :t] if diag else rs[hh][:, :t] + carry_s[hh]
            a = jnp.exp(es[hh] + between)
            if diag:
                a = jnp.where(causal, a, 0.0)
            pvs.append(_dot(a.astype(BF16), v_ref[0, pl.ds(s0, t), hs]))
        for hh in range(hb):
            if diag:
                acc_s[hh] = pvs[hh]
                carry_s[hh] = rs[hh][:, t:]
            else:
                acc_s[hh] += pvs[hh]
                carry_s[hh] += rs[hh][:, t:]

    def any_live():
        m = carry_s[0]
        for hh in range(1, hb):
            m = jnp.maximum(m, carry_s[hh])
        return jnp.max(m) >= SB_DEAD

    blocks(pl.multiple_of(qi * t, t), True)

    def cond(st):
        return (st[0] < qi) & st[1]

    def body(st):
        j = qi - 1 - st[0]
        blocks(pl.multiple_of(j * t, t), False)
        return st[0] + 1, any_live()

    lax.while_loop(cond, body, (jnp.int32(0), any_live()))
    for hh in range(hb):
        o_ref[0, :, hh * HEAD_DIM:(hh + 1) * HEAD_DIM] = acc_s[hh].astype(o_ref.dtype)


def sb_attention(proj, col0):
    B, S, _ = proj.shape
    hb = SB_HEADS_PER_STEP
    n_grp = SB_HEADS // hb
    t = SB_TILE
    w = hb * HEAD_DIM
    c0 = col0 // hb
    return pl.pallas_call(
        _sb_kernel,
        grid=(B * n_grp, S // t),
        in_specs=[pl.BlockSpec((1, t, w), lambda g, i: (g // n_grp, i, c0 + g % n_grp)),
                  pl.BlockSpec((1, S, w), lambda g, i: (g // n_grp, 0, c0 + n_grp + g % n_grp)),
                  pl.BlockSpec((1, S, w), lambda g, i: (g // n_grp, 0, c0 + 2 * n_grp + g % n_grp))],
        out_specs=pl.BlockSpec((1, t, w), lambda g, i: (g // n_grp, i, g % n_grp)),
        out_shape=jax.ShapeDtypeStruct((B, S, SB_HEADS * HEAD_DIM), BF16),
        scratch_shapes=[pltpu.VMEM((hb, t, t), F32), pltpu.VMEM((hb, t, HEAD_DIM), F32)],
        compiler_params=_params(("parallel", "arbitrary")),
        name="sb_attention",
    )(proj, proj, proj)


ROPE_TM = 512


def _rope_kernel(main_ref, small_ref, pos_ref, freq_ref,
                 q_ref, k_ref, vt_ref, qi_ref, kilo_ref, kihi_ref, wi_ref):
    tm = main_ref.shape[0]
    lane = lax.broadcasted_iota(I32, (tm, LANES), 1)
    half_a = HEAD_DIM // ROPE_FRACTION // 2
    half_i = IDX_DIM // ROPE_FRACTION // 2

    ang = pos_ref[...] * freq_ref[...]
    cos_p = jnp.cos(ang)
    sin_p = jnp.sin(ang)

    def table(c, s, period, half):
        p = lane % period
        return c, jnp.where(p >= half, s, 0.0), jnp.where(p < half, -s, 0.0)

    in_attn = lane < 2 * half_a
    tab_a = table(jnp.where(in_attn, cos_p, 1.0), jnp.where(in_attn, sin_p, 0.0), HEAD_DIM, half_a)

    first = lane < 2 * half_i
    second = (lane >= IDX_DIM) & (lane < IDX_DIM + 2 * half_i)

    def spread(v, fill):
        return jnp.where(first, pltpu.roll(v, LANES - 2 * half_a, 1),
                         jnp.where(second, pltpu.roll(v, IDX_DIM - 2 * half_a, 1), fill))

    tab_i = table(spread(cos_p, 1.0), spread(sin_p, 0.0), IDX_DIM, half_i)

    def rot(x, tab, half):
        c, s_up, s_dn = tab
        return x * c + pltpu.roll(x, half, 1) * s_up + pltpu.roll(x, LANES - half, 1) * s_dn

    nq = ATTN_HEADS
    nk = ATTN_KV_HEADS
    for c in range(nq):
        x = main_ref[:, c * LANES:(c + 1) * LANES].astype(F32)
        q_ref[:, c * LANES:(c + 1) * LANES] = rot(x, tab_a, half_a).astype(BF16)
    for c in range(nk):
        x = main_ref[:, (nq + c) * LANES:(nq + c + 1) * LANES].astype(F32)
        k_ref[:, c * LANES:(c + 1) * LANES] = rot(x, tab_a, half_a).astype(BF16)
    vt_ref[0] = main_ref[:, (nq + nk) * LANES:(nq + 2 * nk) * LANES].astype(F32).T.astype(BF16)
    c0 = nq + 2 * nk
    for c in range(IDX_HEADS * IDX_DIM // LANES):
        x = main_ref[:, (c0 + c) * LANES:(c0 + c + 1) * LANES].astype(F32)
        qi_ref[:, c * LANES:(c + 1) * LANES] = rot(x, tab_i, half_i).astype(BF16)
    sm = small_ref[...]
    ki = jnp.where(lane < IDX_DIM, rot(sm, tab_i, half_i), 0.0)
    kilo_ref[...] = ki.astype(BF16)
    kihi_ref[...] = pltpu.roll(ki, IDX_DIM, 1).astype(BF16)
    wi_scale = (IDX_HEADS ** -0.5) * (IDX_DIM ** -0.5)
    wi_ref[...] = jnp.where(lane < IDX_HEADS, pltpu.roll(sm, LANES - IDX_DIM, 1) * wi_scale, 0.0)


def rope_split(main, small, pos_b, freqs, seq_len):
    T = main.shape[0]
    tm = ROPE_TM
    per_seq = seq_len // tm
    row = lambda n: pl.BlockSpec((tm, n), lambda i: (i, 0))
    vec = pl.BlockSpec((1, LANES), lambda i: (0, 0))
    nq, nk = ATTN_HEADS * HEAD_DIM, ATTN_KV_HEADS * HEAD_DIM
    ni = IDX_HEADS * IDX_DIM
    return pl.pallas_call(
        _rope_kernel,
        grid=(T // tm,),
        in_specs=[row(main.shape[1]), row(LANES), row(LANES), vec],
        out_specs=[row(nq), row(nk), pl.BlockSpec((1, nk, tm), lambda i: (i // per_seq, 0, i % per_seq)),
                   row(ni), row(LANES), row(LANES), row(LANES)],
        out_shape=[jax.ShapeDtypeStruct((T, nq), BF16), jax.ShapeDtypeStruct((T, nk), BF16),
                   jax.ShapeDtypeStruct((T // seq_len, nk, seq_len), BF16), jax.ShapeDtypeStruct((T, ni), BF16),
                   jax.ShapeDtypeStruct((T, LANES), BF16), jax.ShapeDtypeStruct((T, LANES), BF16),
                   jax.ShapeDtypeStruct((T, LANES), F32)],
        compiler_params=_params(("parallel",)),
        name="rope_split",
    )(main, small, pos_b, freqs)


DSA_TQ = 256
DSA_TK = 256
INT_MIN = -2 ** 31


def _dsa_kernel(q_ref, k_ref, vt_ref, qi_ref, kilo_ref, kihi_ref, wi_ref, o_ref,
                scores, m_s, l_s, acc_s):
    tq, tk = DSA_TQ, DSA_TK
    S = k_ref.shape[1]
    topk = min(TOPK_MAX, S // TOPK_KEY_FRACTION)
    q0 = pl.program_id(1) * tq
    n_live = (q0 + tq + tk - 1) // tk
    exp2_scale = (HEAD_DIM ** -0.5) * float(np.log2(np.e))

    key_i = lax.broadcasted_iota(I32, (tk, tq), 0)
    qry_i = q0 + lax.broadcasted_iota(I32, (tk, tq), 1)

    def tile_start(kt):
        return pl.multiple_of(kt * tk, tk)

    def key_to_float(key):
        return pltpu.bitcast(jnp.where(key < 0, key ^ 0x7FFFFFFF, key), F32)

    wit = wi_ref[0].T
    qi_pairs = jnp.concatenate([qi_ref[0, :, c * LANES:(c + 1) * LANES]
                                for c in range(IDX_HEADS * IDX_DIM // LANES)], axis=0)

    def score_tile(kt, carry):
        s0 = tile_start(kt)
        even = _dot_nt(kilo_ref[0, pl.ds(s0, tk), :], qi_pairs)
        odd = _dot_nt(kihi_ref[0, pl.ds(s0, tk), :], qi_pairs)
        score = None
        for h in range(IDX_HEADS):
            logit = (even if h % 2 == 0 else odd)[:, (h // 2) * tq:(h // 2 + 1) * tq]
            term = wit[h:h + 1, :] * jnp.maximum(logit, 0.0)
            score = term if score is None else score + term
        scores[pl.ds(s0, tk), :] = jnp.where(s0 + key_i <= qry_i, score, NEG_INF)
        return carry

    lax.fori_loop(0, n_live, score_tile, 0)

    @pl.when(n_live % 2 == 1)
    def _():
        scores[pl.ds(tile_start(n_live), tk), :] = jnp.full((tk, tq), NEG_INF, F32)

    kf = float(topk)
    ts = 2 * tk
    n_sums = 4

    def count(above):
        def tile(kt, acc):
            hit = jnp.where(above(scores[pl.ds(pl.multiple_of(kt * ts, ts), ts), :]), 1.0, 0.0)
            return acc + jnp.sum(hit.reshape(n_sums, ts // (8 * n_sums), 8, tq), axis=1)
        acc = lax.fori_loop(0, (n_live + 1) // 2, tile, jnp.zeros((n_sums, 8, tq), F32))
        return jnp.sum(acc.reshape(n_sums * 8, tq), axis=0, keepdims=True)

    zero = jnp.zeros((1, tq), I32)
    prefix = jnp.where(count(lambda s: s >= 0.0) >= kf, zero, jnp.full((1, tq), INT_MIN, I32))

    def bit_step(i, prefix):
        cand = prefix | jnp.left_shift(jnp.int32(1), 30 - i)
        cand_f = key_to_float(cand)
        return jnp.where(count(lambda s: s >= cand_f) >= kf, cand, prefix)

    thr_key = lax.fori_loop(0, 31, bit_step, prefix)
    neg_inf_key = INT_MIN + 0x7FFFFF
    thr = jnp.where(thr_key < neg_inf_key, NEG_INF, key_to_float(thr_key))
    need = kf - count(lambda s: s > thr)

    m_s[...] = jnp.full(m_s.shape, -1e30, F32)
    l_s[...] = jnp.zeros(l_s.shape, F32)
    acc_s[...] = jnp.zeros(acc_s.shape, F32)
    before = (lax.broadcasted_iota(I32, (tk, tk), 1) < lax.broadcasted_iota(I32, (tk, tk), 0)).astype(BF16)
    groups = range(ATTN_KV_HEADS)
    q4 = [jnp.concatenate([(q_ref[0, :, (g * GQA_GROUP + u) * HEAD_DIM:(g * GQA_GROUP + u + 1) * HEAD_DIM]
                            .astype(F32) * exp2_scale).astype(BF16)
                           for u in range(GQA_GROUP)], axis=0) for g in groups]
    ones_rows = jnp.ones((16, tk), BF16)

    def attn_tile(kt, ties_seen):
        s0 = tile_start(kt)
        score = scores[pl.ds(s0, tk), :]
        eq = score == thr
        rank = _dot(before, jnp.where(eq, 1.0, 0.0).astype(BF16)) + ties_seen
        sel = ((score > thr) | (eq & (rank < need))) & (s0 + key_i <= qry_i)
        bias = jnp.where(sel, 0.0, -1e30).astype(BF16)
        sts = [_dot_nt(k_ref[0, pl.ds(s0, tk), g * HEAD_DIM:(g + 1) * HEAD_DIM], q4[g]) for g in groups]
        ps, alphas = [], []
        for g in groups:
            p_g, a_g = [], []
            for u in range(GQA_GROUP):
                h = g * GQA_GROUP + u
                s = sts[g][:, u * tq:(u + 1) * tq].astype(BF16) + bias
                m_old = m_s[h]
                m_new = jnp.maximum(m_old, jnp.max(s, axis=0, keepdims=True).astype(F32))
                p_g.append(jnp.exp2(s - m_new.astype(BF16)))
                a_g.append(jnp.exp2(m_old - m_new))
                m_s[h] = m_new
            ps.append(jnp.concatenate(p_g, axis=1))
            alphas.append(jnp.concatenate(a_g, axis=1))
        pvs = [_dot(vt_ref[0, g * HEAD_DIM:(g + 1) * HEAD_DIM, pl.ds(s0, tk)], ps[g]) for g in groups]
        sums = [_dot(ones_rows, ps[g])[0:1, :] for g in groups]
        for g in groups:
            acc_s[g] = alphas[g] * acc_s[g] + pvs[g]
            l_s[g] = alphas[g] * l_s[g] + sums[g]
        return ties_seen + jnp.sum(jnp.where(eq, 1.0, 0.0), axis=0, keepdims=True)

    lax.fori_loop(0, n_live, attn_tile, jnp.zeros((1, tq), F32))

    for g in groups:
        for u in range(GQA_GROUP):
            h = g * GQA_GROUP + u
            cols = slice(u * tq, (u + 1) * tq)
            o_t = acc_s[g, :, cols] / l_s[g, :, cols]
            o_ref[0, :, h * HEAD_DIM:(h + 1) * HEAD_DIM] = o_t.T.astype(o_ref.dtype)


def dsa_attention(q, k, vt, qi, ki_lo, ki_hi, wi):
    B, S, _ = q.shape
    tq = DSA_TQ
    qblk = lambda n: pl.BlockSpec((1, tq, n), lambda b, i: (b, i, 0))
    seq = lambda n: pl.BlockSpec((1, S, n), lambda b, i: (b, 0, 0))
    return pl.pallas_call(
        _dsa_kernel,
        grid=(B, S // tq),
        in_specs=[qblk(q.shape[2]), seq(k.shape[2]),
                  pl.BlockSpec((1, vt.shape[1], S), lambda b, i: (b, 0, 0)),
                  qblk(qi.shape[2]), seq(LANES), seq(LANES), qblk(LANES)],
        out_specs=qblk(q.shape[2]),
        out_shape=jax.ShapeDtypeStruct(q.shape, BF16),
        scratch_shapes=[pltpu.VMEM((S, tq), F32),
                        pltpu.VMEM((ATTN_HEADS, 1, tq), F32),
                        pltpu.VMEM((ATTN_KV_HEADS, 1, GQA_GROUP * tq), F32),
                        pltpu.VMEM((ATTN_KV_HEADS, HEAD_DIM, GQA_GROUP * tq), F32)],
        compiler_params=_params(("parallel", "arbitrary")),
        name="dsa_attention",
    )(q, k, vt, qi, ki_lo, ki_hi, wi)


MM_TM = 1024
MM_TN = 512
OUT_PROJ_TM = 512


def _pad_cols(w, n):
    return jnp.pad(w, ((0, 0), (0, n - w.shape[1])))


def _row(v):
    return v.reshape(1, -1).astype(F32)


def _lane_vec(v):
    return jnp.pad(v.astype(F32), (0, LANES - v.shape[0])).reshape(1, LANES)


def _rope_freqs():
    def both_halves(width):
        half = width // ROPE_FRACTION // 2
        inv_freq = ROPE_THETA ** (-jnp.arange(half, dtype=F32) / half)
        return jnp.concatenate([inv_freq, inv_freq])
    f = jnp.concatenate([both_halves(HEAD_DIM), both_halves(IDX_DIM)])
    return jnp.pad(f, (0, LANES - f.shape[0])).reshape(1, LANES)


def _ffn(x, g, w_gate, w_up, w_down, layer):
    u = ffn_gate_up(x, _row(g), w_gate, w_up, layer, MM_TM, MM_TN)
    return matmul_residual(u, w_down[layer].astype(BF16), x, MM_TM, MM_TN)


def _mixer_ab(xf, B, S, norm_g, w_in, conv_w, a_log, dt_bias, gdn_norm, w_out):
    T = B * S
    gw = GDN_HEADS * HEAD_DIM
    sw = SB_HEADS * HEAD_DIM
    n_small0 = 4 * gw
    w_main = jnp.concatenate([w_in[:, :n_small0], w_in[:, n_small0 + 2 * GDN_HEADS:]], axis=1).astype(BF16)
    w_small = _pad_cols(w_in[:, n_small0:n_small0 + 2 * GDN_HEADS], LANES).astype(BF16)
    g0 = _row(norm_g)
    proj, ab = rms_matmul(xf, g0, w_main, w_main.shape[1], w_small, MM_TM, MM_TN)
    proj = proj.reshape(B, S, -1)
    ab = ab.reshape(B, S, LANES)
    o_a = gdn_mixer(proj, ab, conv_w, _lane_vec(a_log), _lane_vec(dt_bias), _row(gdn_norm))
    o_b = sb_attention(proj, (4 * gw) // HEAD_DIM)
    return matmul2_residual(o_a.reshape(T, gw), o_b.reshape(T, sw), w_out.astype(BF16), xf, OUT_PROJ_TM)


def _mixer_c(xf, B, S, positions, norm_g, w_in, w_out):
    T = B * S
    n_main = (ATTN_HEADS + 2 * ATTN_KV_HEADS) * HEAD_DIM + IDX_HEADS * IDX_DIM
    g1 = _row(norm_g)
    main, small = rms_matmul(xf, g1, w_in, n_main, _pad_cols(w_in[:, n_main:], LANES).astype(BF16), MM_TM, MM_TN)
    pos_b = jnp.broadcast_to(positions.reshape(T, 1).astype(F32), (T, LANES))
    q, k, vt, qi, ki_lo, ki_hi, wi = rope_split(main, small, pos_b, _rope_freqs(), S)
    r3 = lambda t: t.reshape(B, S, -1)
    o = dsa_attention(r3(q), r3(k), vt, r3(qi), r3(ki_lo), r3(ki_hi), r3(wi)).reshape(T, -1)
    return matmul_residual(o, w_out.astype(BF16), xf, OUT_PROJ_TM, w_out.shape[1])


def kernel(x, positions, norm_mix, norm_ffn, final_norm, w_in_ab, conv_w_a, a_log, dt_bias, gdn_norm,
           w_out_ab, w_in_c, w_out_c, ffn_gate, ffn_up, ffn_down):
    B, S, D = x.shape
    xf = x.reshape(B * S, D)
    xf = _mixer_ab(xf, B, S, norm_mix[0], w_in_ab[0], conv_w_a[0], a_log[0], dt_bias[0], gdn_norm[0], w_out_ab[0])
    xf = _ffn(xf, norm_ffn[0], ffn_gate, ffn_up, ffn_down, 0)
    xf = _mixer_c(xf, B, S, positions, norm_mix[1], w_in_c[0], w_out_c[0])
    xf = _ffn(xf, norm_ffn[1], ffn_gate, ffn_up, ffn_down, 1)
    return rmsnorm_rows(xf, _row(final_norm), MM_TM).reshape(B, S, D)
```
